```python
import jax, jax.numpy as jnp
from jax import lax
import numpy as np

D_MODEL = 2048
BATCH = 8
SEQ = 4096
DEPTH = 4

N_A_LAYERS = DEPTH // 2
N_B_LAYERS = DEPTH - N_A_LAYERS
D_FF = 4 * D_MODEL
NORM_EPS = 1e-6

GDN_DK = 128
GDN_DV = 128
GDN_NK = D_MODEL // 128
GDN_NV = 2 * GDN_NK
GDN_CONV_K = 4
GDN_CHUNK = 64
GDN_QK_DIM = GDN_NK * GDN_DK
GDN_V_DIM = GDN_NV * GDN_DV
GDN_CONV_DIM = 2 * GDN_QK_DIM + GDN_V_DIM
GDN_PROJ = GDN_CONV_DIM + GDN_V_DIM + 2 * GDN_NV

ATT_HD = 128
ATT_HQ = D_MODEL // ATT_HD
ATT_HKV = 4
DILATED_GROUPS = ((128, 1), (512, 4), (2048, 16))
N_GROUPS = len(DILATED_GROUPS)
ATT_Q_PROJ = N_GROUPS * ATT_HQ * ATT_HD
ATT_KV_PROJ = N_GROUPS * 2 * ATT_HKV * ATT_HD
ROPE_THETA = 500000.0
ROT_DIM = ATT_HD // 4

kernel_name = "yoco_gdn_dilated_hybrid"


def rms_norm(x, gain):
    xf = x.astype(jnp.float32)
    y = xf * lax.rsqrt(jnp.mean(xf * xf, axis=-1, keepdims=True) + NORM_EPS)
    return (y * gain.astype(jnp.float32)).astype(x.dtype)


def modulate(x, gain, shift, scale):
    return rms_norm(x, gain) * (1 + scale[:, None, :]) + shift[:, None, :]


def l2_normalize(t):
    return t * lax.rsqrt(jnp.sum(t * t, axis=-1, keepdims=True) + NORM_EPS)


def partial_rotary(t, positions):
    half = ROT_DIM // 2
    inv = ROPE_THETA ** (-jnp.arange(0, ROT_DIM, 2, dtype=jnp.float32) / ROT_DIM)
    ang = positions.astype(jnp.float32)[..., None] * inv
    cos = jnp.cos(ang)[:, :, None, :]
    sin = jnp.sin(ang)[:, :, None, :]
    t1 = t[..., :half].astype(jnp.float32)
    t2 = t[..., half:ROT_DIM].astype(jnp.float32)
    rot = jnp.concatenate([t1 * cos - t2 * sin, t2 * cos + t1 * sin], axis=-1).astype(t.dtype)
    return jnp.concatenate([rot, t[..., ROT_DIM:]], axis=-1)


def causal_depthwise_conv(x, w):
    k_len, ch = w.shape
    return lax.conv_general_dilated(
        x, w[:, None, :].astype(x.dtype), window_strides=(1,), padding=[(k_len - 1, 0)],
        dimension_numbers=("NWC", "WIO", "NWC"), feature_group_count=ch)


def chunk_gated_delta_rule(q, k, v, g, beta):
    B, H, S, DK = q.shape
    DV = v.shape[-1]
    C = GDN_CHUNK
    N = S // C
    q = (q * DK ** -0.5).reshape(B, H, N, C, DK)
    k = k.reshape(B, H, N, C, DK)
    v = v.reshape(B, H, N, C, DV)
    g = jnp.cumsum(g.reshape(B, H, N, C), axis=-1)
    beta = beta.reshape(B, H, N, C)
    idx = jnp.arange(C)
    causal = idx[:, None] >= idx[None, :]
    strict = idx[:, None] > idx[None, :]
    decay = jnp.exp(jnp.where(causal, g[..., :, None] - g[..., None, :], -jnp.inf))
    kb = k * beta[..., None]
    lmat = jnp.einsum('bhnid,bhnjd->bhnij', kb, k) * jnp.where(strict, decay, 0.0)
    tri = lmat + jnp.eye(C, dtype=lmat.dtype)
    rhs = jnp.concatenate([v * beta[..., None], kb * jnp.exp(g)[..., None]], axis=-1)
    sol = lax.linalg.triangular_solve(tri, rhs, left_side=True, lower=True, unit_diagonal=True)
    u = sol[..., :DV]
    w = sol[..., DV:]

    def step(state, inp):
        q_c, k_c, u_c, w_c, g_c, dec_c = inp
        v_new = u_c - jnp.einsum('bhck,bhkv->bhcv', w_c, state)
        intra = jnp.einsum('bhik,bhjk->bhij', q_c, k_c) * dec_c
        o_c = (jnp.einsum('bhck,bhkv->bhcv', q_c * jnp.exp(g_c)[..., None], state)
               + jnp.einsum('bhij,bhjv->bhiv', intra, v_new))
        g_last = g_c[..., -1:]
        state = (state * jnp.exp(g_last)[..., None]
                 + jnp.einsum('bhck,bhcv->bhkv', k_c * jnp.exp(g_last - g_c)[..., None], v_new))
        return state, o_c

    xs = tuple(jnp.moveaxis(t, 2, 0) for t in (q, k, u, w, g, decay))
    state0 = jnp.zeros((B, H, DK, DV), jnp.float32)
    _, o = lax.scan(step, state0, xs)
    return jnp.moveaxis(o, 0, 2).reshape(B, H, S, DV)


def gated_deltanet_mixer(h, w_in, conv_w, a_log, dt_bias, onorm_g, w_out):
    B, S, _ = h.shape
    proj = h @ w_in
    o1 = GDN_CONV_DIM
    o2 = o1 + GDN_V_DIM
    o3 = o2 + GDN_NV
    qkv = jax.nn.silu(causal_depthwise_conv(proj[..., :o1], conv_w)).astype(jnp.float32)
    z = proj[..., o1:o2].astype(jnp.float32).reshape(B, S, GDN_NV, GDN_DV)
    b_in = proj[..., o2:o3].astype(jnp.float32)
    a_in = proj[..., o3:].astype(jnp.float32)
    rep = GDN_NV // GDN_NK
    q = jnp.repeat(l2_normalize(qkv[..., :GDN_QK_DIM].reshape(B, S, GDN_NK, GDN_DK)), rep, axis=2)
    k = jnp.repeat(l2_normalize(qkv[..., GDN_QK_DIM:2 * GDN_QK_DIM].reshape(B, S, GDN_NK, GDN_DK)), rep, axis=2)
    v = qkv[..., 2 * GDN_QK_DIM:].reshape(B, S, GDN_NV, GDN_DV)
    beta = jax.nn.sigmoid(b_in)
    g = -jnp.exp(a_log.astype(jnp.float32)) * jax.nn.softplus(a_in + dt_bias.astype(jnp.float32))
    o = chunk_gated_delta_rule(q.transpose(0, 2, 1, 3), k.transpose(0, 2, 1, 3),
                               v.transpose(0, 2, 1, 3), g.transpose(0, 2, 1),
                               beta.transpose(0, 2, 1))
    o = o.transpose(0, 2, 1, 3)
    o = rms_norm(o, onorm_g) * jax.nn.silu(z)
    return o.reshape(B, S, GDN_V_DIM).astype(h.dtype) @ w_out


def banded_causal_attention(q, k, v, span):
    N, L, HQ, HD = q.shape
    HKV = k.shape[2]
    G = HQ // HKV
    nb = -(-L // span)
    Lp = nb * span
    padw = ((0, 0), (0, Lp - L), (0, 0), (0, 0))
    qb = jnp.pad(q, padw).reshape(N, nb, span, HKV, G, HD)
    kb = jnp.pad(k, padw).reshape(N, nb, span, HKV, HD)
    vb = jnp.pad(v, padw).reshape(N, nb, span, HKV, HD)
    shift = ((0, 0), (1, 0), (0, 0), (0, 0), (0, 0))
    kk = jnp.concatenate([jnp.pad(kb, shift)[:, :-1], kb], axis=2)
    vv = jnp.concatenate([jnp.pad(vb, shift)[:, :-1], vb], axis=2)
    s = jnp.einsum('nbqhgd,nbkhd->nbhgqk', qb, kk,
                   preferred_element_type=jnp.float32) * (ATT_HD ** -0.5)
    qi = jnp.arange(span)[:, None] + span
    ki = jnp.arange(2 * span)[None, :]
    rel = qi - ki
    band = (rel >= 0) & (rel <= span)
    valid = (jnp.arange(nb)[:, None, None] > 0) | (ki[None] >= span)
    mask = band[None] & valid
    s = jnp.where(mask[None, :, None, None], s, -jnp.inf)
    m = jnp.max(s, axis=-1, keepdims=True)
    p = jnp.exp(s - m)
    den = jnp.sum(p, axis=-1, keepdims=True)
    o = jnp.einsum('nbhgqk,nbkhd->nbqhgd', (p / den).astype(vv.dtype), vv,
                   preferred_element_type=jnp.float32)
    lse = (m + jnp.log(den))[..., 0].transpose(0, 1, 4, 2, 3)
    o = o.reshape(N, Lp, HQ, HD)[:, :L]
    lse = lse.reshape(N, Lp, HQ)[:, :L]
    return o, lse


def dilated_window_attention(q, k, v, dilation, span):
    B, S, HQ, HD = q.shape
    L = S // dilation

    def to_sub(t):
        return t.reshape(B, L, dilation, t.shape[2], HD).transpose(0, 2, 1, 3, 4).reshape(B * dilation, L, t.shape[2], HD)

    o, lse = banded_causal_attention(to_sub(q), to_sub(k), to_sub(v), span)
    o = o.reshape(B, dilation, L, HQ, HD).transpose(0, 2, 1, 3, 4).reshape(B, S, HQ, HD)
    lse = lse.reshape(B, dilation, L, HQ).transpose(0, 2, 1, 3).reshape(B, S, HQ)
    return o, lse


def shared_kv(x, c_act, kv_norm_g, kv_ada_w, kv_ada_b, w_kv, k_norm_g, positions):
    B, S, _ = x.shape
    shift, scale = jnp.split(c_act @ kv_ada_w + kv_ada_b, 2, axis=-1)
    h = modulate(x, kv_norm_g, shift, scale)
    kv = (h @ w_kv).reshape(B, S, N_GROUPS, 2, ATT_HKV, ATT_HD)
    keys = [partial_rotary(rms_norm(kv[:, :, gi, 0], k_norm_g[gi]), positions) for gi in range(N_GROUPS)]
    values = [kv[:, :, gi, 1] for gi in range(N_GROUPS)]
    return keys, values


def dilated_attention_mixer(h, w_q, q_norm_g, w_o, keys, values, positions):
    B, S, _ = h.shape
    q_all = (h @ w_q).reshape(B, S, N_GROUPS, ATT_HQ, ATT_HD)
    outs, lses = [], []
    for gi, (window, dilation) in enumerate(DILATED_GROUPS):
        q = partial_rotary(rms_norm(q_all[:, :, gi], q_norm_g[gi]), positions)
        o, lse = dilated_window_attention(q, keys[gi], values[gi], dilation, window // dilation)
        outs.append(o)
        lses.append(lse)
    wts = jax.nn.softmax(jnp.stack(lses, axis=0), axis=0)
    o = jnp.einsum('gbsh,gbshd->bshd', wts, jnp.stack(outs, axis=0))
    return o.reshape(B, S, ATT_HQ * ATT_HD).astype(h.dtype) @ w_o


def squared_relu_mlp(h, w1, w2):
    return jnp.square(jax.nn.relu(h @ w1)) @ w2


def setup_inputs(seed: int = 0) -> dict:
    key = jax.random.key(seed)
    ks = jax.random.split(key, 24)
    f32 = jnp.float32

    def nrm(k, shape, scale):
        return jax.random.normal(k, shape, f32) * scale

    def gain(k, shape):
        return 1.0 + 0.02 * jax.random.normal(k, shape, f32)

    x = nrm(ks[0], (BATCH, SEQ, D_MODEL), 1.0)
    c = nrm(ks[1], (BATCH, D_MODEL), 1.0)
    positions = (jax.random.randint(ks[2], (BATCH, 1), 0, 1024, dtype=jnp.int32)
                 + jnp.arange(SEQ, dtype=jnp.int32)[None, :])
    ada_w = nrm(ks[3], (DEPTH, D_MODEL, 6 * D_MODEL), 0.5 * D_MODEL ** -0.5)
    ada_b = nrm(ks[4], (DEPTH, 6 * D_MODEL), 0.02)
    norm_g = gain(ks[5], (DEPTH, 2, D_MODEL))
    mlp_w1 = nrm(ks[6], (DEPTH, D_MODEL, D_FF), D_MODEL ** -0.5)
    mlp_w2 = nrm(ks[7], (DEPTH, D_FF, D_MODEL), D_FF ** -0.5)
    gdn_w_in = nrm(ks[8], (N_A_LAYERS, D_MODEL, GDN_PROJ), D_MODEL ** -0.5)
    gdn_conv_w = nrm(ks[9], (N_A_LAYERS, GDN_CONV_K, GDN_CONV_DIM), GDN_CONV_K ** -0.5)
    gdn_a_log = jnp.log(jax.random.uniform(ks[10], (N_A_LAYERS, GDN_NV), f32, 1.0, 16.0))
    dt = jnp.exp(jax.random.uniform(ks[11], (N_A_LAYERS, GDN_NV), f32, np.log(1e-3), np.log(1e-1)))
    gdn_dt_bias = dt + jnp.log(-jnp.expm1(-dt))
    gdn_onorm_g = gain(ks[12], (N_A_LAYERS, GDN_DV))
    gdn_w_out = nrm(ks[13], (N_A_LAYERS, GDN_V_DIM, D_MODEL), GDN_V_DIM ** -0.5)
    kv_norm_g = gain(ks[14], (D_MODEL,))
    kv_ada_w = nrm(ks[15], (D_MODEL, 2 * D_MODEL), 0.5 * D_MODEL ** -0.5)
    kv_ada_b = nrm(ks[16], (2 * D_MODEL,), 0.02)
    w_kv = nrm(ks[17], (D_MODEL, ATT_KV_PROJ), D_MODEL ** -0.5)
    k_norm_g = gain(ks[18], (N_GROUPS, ATT_HD))
    attn_w_q = nrm(ks[19], (N_B_LAYERS, D_MODEL, ATT_Q_PROJ), D_MODEL ** -0.5)
    q_norm_g = gain(ks[20], (N_B_LAYERS, N_GROUPS, ATT_HD))
    attn_w_o = nrm(ks[21], (N_B_LAYERS, ATT_HQ * ATT_HD, D_MODEL), (ATT_HQ * ATT_HD) ** -0.5)
    return {"x": x, "c": c, "positions": positions, "ada_w": ada_w, "ada_b": ada_b,
            "norm_g": norm_g, "mlp_w1": mlp_w1, "mlp_w2": mlp_w2,
            "gdn_w_in": gdn_w_in, "gdn_conv_w": gdn_conv_w, "gdn_a_log": gdn_a_log,
            "gdn_dt_bias": gdn_dt_bias, "gdn_onorm_g": gdn_onorm_g, "gdn_w_out": gdn_w_out,
            "kv_norm_g": kv_norm_g, "kv_ada_w": kv_ada_w, "kv_ada_b": kv_ada_b, "w_kv": w_kv,
            "k_norm_g": k_norm_g, "attn_w_q": attn_w_q, "q_norm_g": q_norm_g, "attn_w_o": attn_w_o}


def reference(x, c, positions, ada_w, ada_b, norm_g, mlp_w1, mlp_w2,
              gdn_w_in, gdn_conv_w, gdn_a_log, gdn_dt_bias, gdn_onorm_g, gdn_w_out,
              kv_norm_g, kv_ada_w, kv_ada_b, w_kv, k_norm_g, attn_w_q, q_norm_g, attn_w_o):
    c_act = jax.nn.silu(c)
    keys, values = None, None
    for layer in range(DEPTH):
        mod = c_act @ ada_w[layer] + ada_b[layer]
        sh1, sc1, gt1, sh2, sc2, gt2 = jnp.split(mod, 6, axis=-1)
        if layer < N_A_LAYERS:
            h = modulate(x, norm_g[layer, 0], sh1, sc1)
            y = gated_deltanet_mixer(h, gdn_w_in[layer], gdn_conv_w[layer], gdn_a_log[layer],
                                     gdn_dt_bias[layer], gdn_onorm_g[layer], gdn_w_out[layer])
        else:
            if layer == N_A_LAYERS:
                keys, values = shared_kv(x, c_act, kv_norm_g, kv_ada_w, kv_ada_b, w_kv,
                                         k_norm_g, positions)
            j = layer - N_A_LAYERS
            h = modulate(x, norm_g[layer, 0], sh1, sc1)
            y = dilated_attention_mixer(h, attn_w_q[j], q_norm_g[j], attn_w_o[j],
                                        keys, values, positions)
        x = x + gt1[:, None, :] * y
        h = modulate(x, norm_g[layer, 1], sh2, sc2)
        x = x + gt2[:, None, :] * squared_relu_mlp(h, mlp_w1[layer], mlp_w2[layer])
    return x
```

```python
import functools

import numpy as np
import jax
import jax.numpy as jnp
from jax import lax
from jax.experimental import pallas as pl
from jax.experimental.pallas import tpu as pltpu

F32 = jnp.float32
BF16 = jnp.bfloat16

NORM_EPS = 1e-6
HEAD = 128
GDN_NK = 16
GDN_NV = 32
GDN_CONV_K = 4
CHUNK = 64
GDN_QK = GDN_NK * HEAD
GDN_V = GDN_NV * HEAD
GDN_CONV = 2 * GDN_QK + GDN_V
GDN_BA_COL = GDN_CONV + GDN_V
GDN_PROJ_PAD = GDN_BA_COL + 512
ATT_HQ = 16
ATT_HKV = 4
ATT_G = ATT_HQ // ATT_HKV
DILATIONS = (1, 4, 16)
SPAN = 128
N_GROUPS = 3
ROPE_THETA = 500000.0
ROT_DIM = HEAD // 4
NEG_BIG = -1e30

VMEM_LIMIT = 56 * 1024 * 1024


def _cparams(*sem):
    return pltpu.CompilerParams(dimension_semantics=sem, vmem_limit_bytes=VMEM_LIMIT)


def _silu(v):
    return v * jax.nn.sigmoid(v)


def _ada_kernel(c_ref, w_ref, b_ref, o_ref):
    c = c_ref[...]
    ca = _silu(c).astype(BF16)
    w = w_ref[0].astype(BF16)
    o_ref[0] = jnp.dot(ca, w, preferred_element_type=F32) + b_ref[0]


def ada_modulation(c, w, b, tn=1024):
    nl, d, n = w.shape
    bsz = c.shape[0]
    return pl.pallas_call(
        _ada_kernel,
        grid=(nl, n // tn),
        in_specs=[
            pl.BlockSpec((bsz, d), lambda l, j: (0, 0)),
            pl.BlockSpec((1, d, tn), lambda l, j: (l, 0, j)),
            pl.BlockSpec((1, 1, tn), lambda l, j: (l, 0, j)),
        ],
        out_specs=pl.BlockSpec((1, bsz, tn), lambda l, j: (l, 0, j)),
        out_shape=jax.ShapeDtypeStruct((nl, bsz, n), F32),
        compiler_params=_cparams("parallel", "parallel"),
        name="ada_modulation",
    )(c, w, b.reshape(nl, 1, n))


def _rope_kernel(pos_ref, inv_ref, cos_ref, sin_ref):
    ang = pos_ref[...].astype(F32) * inv_ref[...]
    lane = lax.broadcasted_iota(jnp.int32, ang.shape, 1)
    c = jnp.cos(ang)
    s = jnp.sin(ang)
    half = ROT_DIM // 2
    cos_ref[...] = jnp.where(lane < ROT_DIM, c, 1.0)
    sin_ref[...] = jnp.where(lane < half, -s, jnp.where(lane < ROT_DIM, s, 0.0))


def rope_tables(positions, ts=2048):
    m = positions.size
    half = ROT_DIM // 2
    inv = ROPE_THETA ** (-np.arange(0, ROT_DIM, 2, dtype=np.float32) / ROT_DIM)
    inv_row = np.zeros((1, HEAD), np.float32)
    inv_row[0, :half] = inv
    inv_row[0, half:ROT_DIM] = inv
    ts = min(ts, m)
    return pl.pallas_call(
        _rope_kernel,
        grid=(m // ts,),
        in_specs=[pl.BlockSpec((ts, 1), lambda i: (i, 0)),
                  pl.BlockSpec((1, HEAD), lambda i: (0, 0))],
        out_specs=[pl.BlockSpec((ts, HEAD), lambda i: (i, 0))] * 2,
        out_shape=[jax.ShapeDtypeStruct((m, HEAD), F32)] * 2,
        compiler_params=_cparams("parallel"),
        name="rope_tables",
    )(positions.reshape(m, 1), jnp.asarray(inv_row))


def _swap_rot_halves(t):
    lane = lax.broadcasted_iota(jnp.int32, t.shape, 1)
    half = ROT_DIM // 2
    return jnp.where(lane < half, pltpu.roll(t, HEAD - half, 1), pltpu.roll(t, half, 1))


def _head_norm_rope(t, gain, cos_t, sin_t):
    y = t * lax.rsqrt(jnp.mean(t * t, axis=-1, keepdims=True) + NORM_EPS) * gain
    return y * cos_t + _swap_rot_halves(y) * sin_t


def _norm_modulate(x, gain, shift, scale):
    y = x * lax.rsqrt(jnp.mean(x * x, axis=-1, keepdims=True) + NORM_EPS) * gain
    return y * (1.0 + scale) + shift


def _nmm_kernel(x_ref, sh_ref, sc_ref, g_ref, w_ref, *rest, rope_every):
    if rope_every:
        cos_ref, sin_ref, hg_ref, o_ref, h_ref = rest
    else:
        o_ref, h_ref = rest
    j = pl.program_id(1)

    @pl.when(j == 0)
    def _():
        h_ref[...] = _norm_modulate(x_ref[...], g_ref[...], sh_ref[0], sc_ref[0]).astype(BF16)

    acc = jnp.dot(h_ref[...], w_ref[...], preferred_element_type=F32)
    if not rope_every:
        o_ref[...] = acc.astype(o_ref.dtype)
        return

    def plain():
        o_ref[...] = acc.astype(o_ref.dtype)

    def roped():
        cos_t = cos_ref[...]
        sin_t = sin_ref[...]
        gain = hg_ref[0]
        for hh in range(acc.shape[1] // HEAD):
            sl = slice(hh * HEAD, (hh + 1) * HEAD)
            o_ref[:, sl] = _head_norm_rope(acc[:, sl], gain, cos_t, sin_t).astype(o_ref.dtype)

    if rope_every == 1:
        roped()
    else:
        pl.when(j % rope_every == 0)(roped)
        pl.when(j % rope_every != 0)(plain)


def norm_mod_matmul(x, gain, shift, scale, w, seq, *, out_dtype, tm=1024, tn=512,
                    rope=None):
    m, d = x.shape
    n = w.shape[1]
    bsz = shift.shape[0]
    tpb = seq // tm
    in_specs = [
        pl.BlockSpec((tm, d), lambda i, j: (i, 0)),
        pl.BlockSpec((1, 1, d), lambda i, j: (i // tpb, 0, 0)),
        pl.BlockSpec((1, 1, d), lambda i, j: (i // tpb, 0, 0)),
        pl.BlockSpec((1, d), lambda i, j: (0, 0)),
        pl.BlockSpec((d, tn), lambda i, j: (0, j)),
    ]
    args = [x, shift.reshape(bsz, 1, d), scale.reshape(bsz, 1, d), gain.reshape(1, d), w]
    rope_every = 0
    if rope is not None:
        cos_tab, sin_tab, head_gain, tiles_per_group, rope_every = rope
        ng = head_gain.shape[0]
        in_specs += [
            pl.BlockSpec((tm, HEAD), lambda i, j: (i, 0)),
            pl.BlockSpec((tm, HEAD), lambda i, j: (i, 0)),
            pl.BlockSpec((1, 1, HEAD), lambda i, j: (j // tiles_per_group, 0, 0)),
        ]
        args += [cos_tab, sin_tab, head_gain.reshape(ng, 1, HEAD)]
    return pl.pallas_call(
        functools.partial(_nmm_kernel, rope_every=rope_every),
        grid=(m // tm, n // tn),
        in_specs=in_specs,
        out_specs=pl.BlockSpec((tm, tn), lambda i, j: (i, j)),
        out_shape=jax.ShapeDtypeStruct((m, n), out_dtype),
        scratch_shapes=[pltpu.VMEM((tm, d), BF16)],
        compiler_params=_cparams("parallel", "arbitrary"),
        name="norm_mod_matmul",
    )(*args)


def _mm_res_kernel(a_ref, w_ref, x_ref, gt_ref, o_ref):
    y = jnp.dot(a_ref[...], w_ref[...], preferred_element_type=F32)
    o_ref[...] = x_ref[...] + gt_ref[0] * y


def matmul_gated_residual(a, w, x, gate, seq, *, tm=1024, tn=512):
    m, k = a.shape
    d = w.shape[1]
    bsz = gate.shape[0]
    tpb = seq // tm
    return pl.pallas_call(
        _mm_res_kernel,
        grid=(m // tm, d // tn),
        in_specs=[
            pl.BlockSpec((tm, k), lambda i, j: (i, 0)),
            pl.BlockSpec((k, tn), lambda i, j: (0, j)),
            pl.BlockSpec((tm, tn), lambda i, j: (i, j)),
            pl.BlockSpec((1, 1, tn), lambda i, j: (i // tpb, 0, j)),
        ],
        out_specs=pl.BlockSpec((tm, tn), lambda i, j: (i, j)),
        out_shape=jax.ShapeDtypeStruct((m, d), F32),
        compiler_params=_cparams("parallel", "parallel"),
        name="matmul_gated_residual",
    )(a, w, x, gate.reshape(bsz, 1, d))


def _merge_mm_res_kernel(o0_ref, o1_ref, o2_ref, l0_ref, l1_ref, l2_ref, w_ref, x_ref, gt_ref,
                         o_ref, a_ref):
    j = pl.program_id(1)

    @pl.when(j == 0)
    def _():
        l0 = l0_ref[...]
        l1 = l1_ref[...]
        l2 = l2_ref[...]
        mx = jnp.maximum(jnp.maximum(l0, l1), l2)
        e0 = jnp.exp(l0 - mx)
        e1 = jnp.exp(l1 - mx)
        e2 = jnp.exp(l2 - mx)
        den = e0 + e1 + e2
        w0 = e0 / den
        w1 = e1 / den
        w2 = e2 / den
        for hq in range(ATT_HQ):
            sl = slice(hq * HEAD, (hq + 1) * HEAD)
            c = slice(hq, hq + 1)
            merged = (w0[:, c] * o0_ref[:, sl].astype(F32)
                      + w1[:, c] * o1_ref[:, sl].astype(F32)
                      + w2[:, c] * o2_ref[:, sl].astype(F32))
            a_ref[:, sl] = merged.astype(BF16)

    y = jnp.dot(a_ref[...], w_ref[...], preferred_element_type=F32)
    o_ref[...] = x_ref[...] + gt_ref[0] * y


def merge_matmul_gated_residual(outs, lses, w, x, gate, seq, *, tm=512, tn=512):
    m, k = outs[0].shape
    d = w.shape[1]
    bsz = gate.shape[0]
    tpb = seq // tm
    o_spec = pl.BlockSpec((tm, k), lambda i, j: (i, 0))
    l_spec = pl.BlockSpec((tm, HEAD), lambda i, j: (i, 0))
    return pl.pallas_call(
        _merge_mm_res_kernel,
        grid=(m // tm, d // tn),
        in_specs=[o_spec, o_spec, o_spec, l_spec, l_spec, l_spec,
                  pl.BlockSpec((k, tn), lambda i, j: (0, j)),
                  pl.BlockSpec((tm, tn), lambda i, j: (i, j)),
                  pl.BlockSpec((1, 1, tn), lambda i, j: (i // tpb, 0, j))],
        out_specs=pl.BlockSpec((tm, tn), lambda i, j: (i, j)),
        out_shape=jax.ShapeDtypeStruct((m, d), F32),
        scratch_shapes=[pltpu.VMEM((tm, k), BF16)],
        compiler_params=_cparams("parallel", "arbitrary"),
        name="merge_matmul_gated_residual",
    )(*outs, *lses, w, x, gate.reshape(bsz, 1, d))


def _mlp_kernel(x_ref, sh_ref, sc_ref, gt_ref, g_ref, w1_ref, w2_ref, o_ref, h_ref, acc_ref):
    f = pl.program_id(1)

    @pl.when(f == 0)
    def _():
        h_ref[...] = _norm_modulate(x_ref[...], g_ref[...], sh_ref[0], sc_ref[0]).astype(BF16)

    a = jnp.dot(h_ref[...], w1_ref[...], preferred_element_type=F32)
    a = jnp.square(jnp.maximum(a, 0.0)).astype(BF16)
    part = jnp.dot(a, w2_ref[...], preferred_element_type=F32)

    @pl.when(f == 0)
    def _():
        acc_ref[...] = part

    @pl.when(f > 0)
    def _():
        acc_ref[...] += part

    @pl.when(f == pl.num_programs(1) - 1)
    def _():
        o_ref[...] = x_ref[...] + gt_ref[0] * acc_ref[...]


def mlp_sublayer(x, gain, shift, scale, gate, w1, w2, seq, *, tm=512, tf=1024):
    m, d = x.shape
    dff = w1.shape[1]
    bsz = gate.shape[0]
    tpb = seq // tm
    vec = pl.BlockSpec((1, 1, d), lambda i, f: (i // tpb, 0, 0))
    return pl.pallas_call(
        _mlp_kernel,
        grid=(m // tm, dff // tf),
        in_specs=[
            pl.BlockSpec((tm, d), lambda i, f: (i, 0)),
            vec, vec, vec,
            pl.BlockSpec((1, d), lambda i, f: (0, 0)),
            pl.BlockSpec((d, tf), lambda i, f: (0, f)),
            pl.BlockSpec((tf, d), lambda i, f: (f, 0)),
        ],
        out_specs=pl.BlockSpec((tm, d), lambda i, f: (i, 0)),
        out_shape=jax.ShapeDtypeStruct((m, d), F32),
        scratch_shapes=[pltpu.VMEM((tm, d), BF16), pltpu.VMEM((tm, d), F32)],
        compiler_params=_cparams("parallel", "arbitrary"),
        name="mlp_sublayer",
    )(x, shift.reshape(bsz, 1, d), scale.reshape(bsz, 1, d), gate.reshape(bsz, 1, d),
      gain.reshape(1, d), w1, w2)


def _softplus(v):
    return jnp.maximum(v, 0.0) + jnp.log1p(jnp.exp(-jnp.abs(v)))


def _gates_kernel(ba_ref, alog_ref, dtb_ref, tril_ref, g_ref, gt_ref):
    ba = ba_ref[...]
    lane = lax.broadcasted_iota(jnp.int32, ba.shape, 1)
    is_alpha = (lane >= GDN_NV) & (lane < 2 * GDN_NV)
    g = jnp.where(is_alpha, -jnp.exp(alog_ref[...]) * _softplus(ba + dtb_ref[...]), 0.0)
    gc = jnp.dot(tril_ref[...], g, preferred_element_type=F32, precision=lax.Precision.HIGHEST)
    out = jnp.where(lane < GDN_NV, jax.nn.sigmoid(ba), gc)
    g_ref[...] = out
    gt_ref[...] = out.T


def gdn_gates(proj, a_log, dt_bias, tg=512):
    m = proj.shape[0]
    tg = min(tg, m)
    alog_row = jnp.zeros((1, HEAD), F32).at[0, GDN_NV:2 * GDN_NV].set(a_log)
    dtb_row = jnp.zeros((1, HEAD), F32).at[0, GDN_NV:2 * GDN_NV].set(dt_bias)
    idx = np.arange(tg)
    tril = ((idx[:, None] >= idx[None, :]) & (idx[:, None] // CHUNK == idx[None, :] // CHUNK))
    return pl.pallas_call(
        _gates_kernel,
        grid=(m // tg,),
        in_specs=[pl.BlockSpec((tg, HEAD), lambda i: (i, GDN_BA_COL // HEAD)),
                  pl.BlockSpec((1, HEAD), lambda i: (0, 0)),
                  pl.BlockSpec((1, HEAD), lambda i: (0, 0)),
                  pl.BlockSpec((tg, tg), lambda i: (0, 0))],
        out_specs=[pl.BlockSpec((tg, HEAD), lambda i: (i, 0)),
                   pl.BlockSpec((HEAD, tg), lambda i: (0, i))],
        out_shape=[jax.ShapeDtypeStruct((m, HEAD), F32), jax.ShapeDtypeStruct((HEAD, m), F32)],
        compiler_params=_cparams("parallel"),
        name="gdn_gates",
    )(proj, alog_row, dtb_row, jnp.asarray(tril.astype(np.float32)))


def _dot_t(a, b):
    return lax.dot_general(a, b, (((1,), (1,)), ((), ())), preferred_element_type=F32)


def _tdot(a, b):
    return lax.dot_general(a, b, (((0,), (0,)), ((), ())), preferred_element_type=F32)


def _bmm(a, b):
    return jnp.einsum("bij,bjk->bik", a.astype(BF16), b.astype(BF16),
                      preferred_element_type=F32)


def _bmm_t(a, b):
    return jnp.einsum("bik,bjk->bij", a.astype(BF16), b.astype(BF16),
                      preferred_element_type=F32)


def _unit_lower_inverse(lmat):
    i = lax.broadcasted_iota(jnp.int32, (CHUNK, CHUNK), 0)
    j = lax.broadcasted_iota(jnp.int32, (CHUNK, CHUNK), 1)

    def sub_blocks(s):
        return (((i >> s) & 1) == 1) & ((j >> s) == (i >> s) - 1)

    inv = jnp.where(i == j, 1.0, 0.0) - jnp.where(sub_blocks(0), lmat, 0.0)
    s = 1
    while (1 << s) < CHUNK:
        inv = inv - _bmm(_bmm(inv, jnp.where(sub_blocks(s), lmat, 0.0)), inv)
        s += 1
    return inv


def _gdn_kernel(q_ref, k_ref, v_ref, z_ref, wq_ref, wk_ref, wv_ref, g_ref, gt_ref, og_ref,
                o_ref, qbuf, kbuf, vbuf, state_ref, *, tt, hg):
    hgi = pl.program_id(1)
    t = pl.program_id(2)
    nc = tt // CHUNK
    nh = 2 * hg

    @pl.when(t == 0)
    def _():
        state_ref[...] = jnp.zeros_like(state_ref)
        qbuf[pl.ds(tt, 8), :] = jnp.zeros((8, hg * HEAD), F32)
        kbuf[pl.ds(tt, 8), :] = jnp.zeros((8, hg * HEAD), F32)
        vbuf[pl.ds(tt, 8), :] = jnp.zeros((8, nh * HEAD), F32)

    def conv_silu(x_ref, buf, w_ref):
        buf[pl.ds(0, 8), :] = buf[pl.ds(tt, 8), :]
        buf[pl.ds(8, tt), :] = x_ref[0]
        acc = w_ref[GDN_CONV_K - 1:GDN_CONV_K, :] * buf[pl.ds(8, tt), :]
        for j in range(1, GDN_CONV_K):
            tap = GDN_CONV_K - 1 - j
            acc = acc + w_ref[tap:tap + 1, :] * buf[pl.ds(8 - j, tt), :]
        return _silu(acc)

    def l2n(v):
        return v * lax.rsqrt(jnp.sum(v * v, axis=-1, keepdims=True) + NORM_EPS)

    def chunked(v):
        return v.reshape(nc, CHUNK, HEAD)

    qa = conv_silu(q_ref, qbuf, wq_ref)
    ka = conv_silu(k_ref, kbuf, wk_ref)
    va = conv_silu(v_ref, vbuf, wv_ref)

    gates = g_ref[0]
    lane = lax.broadcasted_iota(jnp.int32, gates.shape, 1)
    row_i = lax.broadcasted_iota(jnp.int32, (CHUNK, CHUNK), 0)
    col_j = lax.broadcasted_iota(jnp.int32, (CHUNK, CHUNK), 1)
    causal = row_i >= col_j
    strict = row_i > col_j

    lmats, rhss, qgs, kgs, pmats, elasts = [], [], [], [], [], []
    for khl in range(hg):
        hs = slice(khl * HEAD, (khl + 1) * HEAD)
        qc = chunked(l2n(qa[:, hs]) * (HEAD ** -0.5))
        kc = chunked(l2n(ka[:, hs]))
        kk = _bmm_t(kc, kc)
        qk = _bmm_t(qc, kc)
        for e in range(2):
            hl = 2 * khl + e
            hv = hgi * nh + hl
            beta_col = jnp.sum(jnp.where(lane == hv, gates, 0.0), axis=-1, keepdims=True)
            gc_col = jnp.sum(jnp.where(lane == hv + GDN_NV, gates, 0.0), axis=-1, keepdims=True)
            gc_row = gt_ref[pl.ds(hv + GDN_NV, 1), :]
            bi = beta_col.reshape(nc, CHUNK, 1)
            gci = gc_col.reshape(nc, CHUNK, 1)
            gcj = jnp.stack([gc_row[:, c * CHUNK:(c + 1) * CHUNK] for c in range(nc)], axis=0)
            dec = jnp.exp(jnp.where(causal, gci - gcj, NEG_BIG))
            egc = jnp.exp(gci)
            g_last = gci[:, CHUNK - 1:CHUNK, :]
            vc = chunked(va[:, hl * HEAD:(hl + 1) * HEAD])
            lmats.append(jnp.where(strict, kk * bi * dec, 0.0))
            rhss.append(jnp.concatenate([vc * bi, kc * (bi * egc)], axis=-1))
            qgs.append((qc * egc).astype(BF16))
            kgs.append((kc * jnp.exp(g_last - gci)).astype(BF16))
            pmats.append((qk * dec).astype(BF16))
            elasts.append(jnp.exp(g_last))
    inv = _unit_lower_inverse(jnp.concatenate(lmats, axis=0))
    sol = _bmm(inv, jnp.concatenate(rhss, axis=0))

    states = [state_ref[hl] for hl in range(nh)]
    o_chunks = [[] for _ in range(nh)]
    for c in range(nc):
        for hl in range(nh):
            b = hl * nc + c
            u, w = sol[b, :, :HEAD], sol[b, :, HEAD:]
            ws_qs = jnp.dot(jnp.concatenate([w.astype(BF16), qgs[hl][c]], axis=0),
                            states[hl].astype(BF16), preferred_element_type=F32)
            vnb = (u - ws_qs[:CHUNK]).astype(BF16)
            o_chunks[hl].append(
                ws_qs[CHUNK:] + jnp.dot(pmats[hl][c], vnb, preferred_element_type=F32))
            states[hl] = states[hl] * elasts[hl][c] + _tdot(kgs[hl][c], vnb)
    for hl in range(nh):
        state_ref[hl] = states[hl]
        o = jnp.concatenate(o_chunks[hl], axis=0)
        z = z_ref[0, :, hl * HEAD:(hl + 1) * HEAD]
        y = o * lax.rsqrt(jnp.mean(o * o, axis=-1, keepdims=True) + NORM_EPS) * og_ref[...]
        o_ref[0, :, hl * HEAD:(hl + 1) * HEAD] = (y * _silu(z)).astype(BF16)


def gdn_core(proj, conv_w, gates, gates_t, onorm_g, bsz, seq, *, tt=256, hg=2):
    tt = min(tt, seq)
    nt = seq // tt
    nh = 2 * hg
    proj3 = proj.reshape(bsz, seq, proj.shape[1])
    qw = hg * HEAD
    vw = nh * HEAD
    kq = GDN_QK // qw
    kv = 2 * GDN_QK // vw
    kz = GDN_CONV // vw
    out = pl.pallas_call(
        functools.partial(_gdn_kernel, tt=tt, hg=hg),
        grid=(bsz, GDN_NK // hg, nt),
        in_specs=[
            pl.BlockSpec((1, tt, qw), lambda b, h, t: (b, t, h)),
            pl.BlockSpec((1, tt, qw), lambda b, h, t: (b, t, kq + h)),
            pl.BlockSpec((1, tt, vw), lambda b, h, t: (b, t, kv + h)),
            pl.BlockSpec((1, tt, vw), lambda b, h, t: (b, t, kz + h)),
            pl.BlockSpec((GDN_CONV_K, qw), lambda b, h, t: (0, h)),
            pl.BlockSpec((GDN_CONV_K, qw), lambda b, h, t: (0, kq + h)),
            pl.BlockSpec((GDN_CONV_K, vw), lambda b, h, t: (0, kv + h)),
            pl.BlockSpec((1, tt, HEAD), lambda b, h, t: (b, t, 0)),
            pl.BlockSpec((HEAD, tt), lambda b, h, t: (0, b * nt + t)),
            pl.BlockSpec((1, HEAD), lambda b, h, t: (0, 0)),
        ],
        out_specs=pl.BlockSpec((1, tt, vw), lambda b, h, t: (b, t, h)),
        out_shape=jax.ShapeDtypeStruct((bsz, seq, GDN_V), BF16),
        scratch_shapes=[pltpu.VMEM((tt + 8, qw), F32), pltpu.VMEM((tt + 8, qw), F32),
                        pltpu.VMEM((tt + 8, vw), F32), pltpu.VMEM((nh, HEAD, HEAD), F32)],
        compiler_params=_cparams("parallel", "parallel", "arbitrary"),
        name="gdn_core",
    )(proj3, proj3, proj3, proj3, conv_w, conv_w, conv_w,
      gates.reshape(bsz, seq, HEAD), gates_t, onorm_g.reshape(1, HEAD))
    return out.reshape(bsz * seq, GDN_V)


def _attn_kernel(q_ref, kp_ref, kc_ref, vp_ref, vc_ref, o_ref, lse_ref):
    lb = pl.program_id(2)
    rows = ATT_G * SPAN
    qi = lax.broadcasted_iota(jnp.int32, (rows, 2 * SPAN), 0) & (SPAN - 1)
    kj = lax.broadcasted_iota(jnp.int32, (rows, 2 * SPAN), 1)
    rel = qi + SPAN - kj
    mask = (rel >= 0) & (rel <= SPAN) & (kj >= jnp.where(lb > 0, 0, SPAN))
    scale = HEAD ** -0.5
    lse_ref[0] = jnp.zeros(lse_ref.shape[1:], F32)
    for h in range(ATT_HKV):
        hs = slice(h * HEAD, (h + 1) * HEAD)
        k2 = jnp.concatenate([kp_ref[0, :, hs], kc_ref[0, :, hs]], axis=0)
        v2 = jnp.concatenate([vp_ref[0, :, hs], vc_ref[0, :, hs]], axis=0)
        q4 = jnp.concatenate(
            [q_ref[0, :, (h * ATT_G + g) * HEAD:(h * ATT_G + g + 1) * HEAD] for g in range(ATT_G)],
            axis=0)
        s = jnp.where(mask, _dot_t(q4, k2) * scale, NEG_BIG)
        mx = jnp.max(s, axis=-1, keepdims=True)
        p = jnp.exp(s - mx)
        den = jnp.sum(p, axis=-1, keepdims=True)
        o4 = jnp.dot(p.astype(BF16), v2, preferred_element_type=F32) / den
        lse = mx + jnp.log(den)
        for g in range(ATT_G):
            hq = h * ATT_G + g
            o_ref[0, :, hq * HEAD:(hq + 1) * HEAD] = o4[g * SPAN:(g + 1) * SPAN].astype(o_ref.dtype)
            lse_ref[0, :, hq:hq + 1] = lse[g * SPAN:(g + 1) * SPAN]


def dilated_attention_branch(q, kv, gi, bsz, seq):
    dil = DILATIONS[gi]
    sub = seq // dil
    nb = sub // SPAN
    cq = q.shape[1]
    ckv = kv.shape[1]
    qd = ATT_HQ * HEAD
    kd = ATT_HKV * HEAD
    q3 = q.reshape(bsz, sub, dil * cq)
    kv3 = kv.reshape(bsz, sub, dil * ckv)
    qpb = cq // qd
    kpb = ckv // kd

    def prev(lb):
        return jnp.maximum(lb - 1, 0)

    o, lse = pl.pallas_call(
        _attn_kernel,
        grid=(bsz, dil, nb),
        in_specs=[
            pl.BlockSpec((1, SPAN, qd), lambda b, r, lb: (b, lb, r * qpb + gi)),
            pl.BlockSpec((1, SPAN, kd), lambda b, r, lb: (b, prev(lb), r * kpb + 2 * gi)),
            pl.BlockSpec((1, SPAN, kd), lambda b, r, lb: (b, lb, r * kpb + 2 * gi)),
            pl.BlockSpec((1, SPAN, kd), lambda b, r, lb: (b, prev(lb), r * kpb + 2 * gi + 1)),
            pl.BlockSpec((1, SPAN, kd), lambda b, r, lb: (b, lb, r * kpb + 2 * gi + 1)),
        ],
        out_specs=[pl.BlockSpec((1, SPAN, qd), lambda b, r, lb: (b, lb, r)),
                   pl.BlockSpec((1, SPAN, HEAD), lambda b, r, lb: (b, lb, r))],
        out_shape=[jax.ShapeDtypeStruct((bsz, sub, dil * qd), BF16),
                   jax.ShapeDtypeStruct((bsz, sub, dil * HEAD), F32)],
        compiler_params=_cparams("parallel", "parallel", "parallel"),
        name=f"dilated_attention_{gi}",
    )(q3, kv3, kv3, kv3, kv3)
    return o.reshape(bsz * seq, qd), lse.reshape(bsz * seq, HEAD)


def kernel(x, c, positions, ada_w, ada_b, norm_g, mlp_w1, mlp_w2, gdn_w_in, gdn_conv_w, gdn_a_log,
           gdn_dt_bias, gdn_onorm_g, gdn_w_out, kv_norm_g, kv_ada_w, kv_ada_b, w_kv, k_norm_g,
           attn_w_q, q_norm_g, attn_w_o):
    bsz, seq, d = x.shape
    depth = ada_w.shape[0]
    n_a = gdn_w_in.shape[0]
    m = bsz * seq
    xf = x.reshape(m, d)

    mod = ada_modulation(c, ada_w, ada_b)
    kv_mod = ada_modulation(c, kv_ada_w[None], kv_ada_b[None])[0]
    cos_tab, sin_tab = rope_tables(positions)

    kv = None
    for layer in range(depth):
        sh1, sc1, gt1, sh2, sc2, gt2 = [mod[layer, :, i * d:(i + 1) * d] for i in range(6)]
        if layer < n_a:
            w_in = jnp.pad(gdn_w_in[layer], ((0, 0), (0, GDN_PROJ_PAD - gdn_w_in.shape[2])))
            proj = norm_mod_matmul(xf, norm_g[layer, 0], sh1, sc1, w_in.astype(BF16), seq,
                                   out_dtype=F32)
            gates, gates_t = gdn_gates(proj, gdn_a_log[layer], gdn_dt_bias[layer])
            o = gdn_core(proj, gdn_conv_w[layer], gates, gates_t, gdn_onorm_g[layer], bsz, seq)
            xf = matmul_gated_residual(o, gdn_w_out[layer].astype(BF16), xf, gt1, seq)
        else:
            if kv is None:
                kv = norm_mod_matmul(xf, kv_norm_g, kv_mod[:, :d], kv_mod[:, d:],
                                     w_kv.astype(BF16), seq, out_dtype=BF16,
                                     rope=(cos_tab, sin_tab, k_norm_g, 2, 2))
            j = layer - n_a
            q = norm_mod_matmul(xf, norm_g[layer, 0], sh1, sc1, attn_w_q[j].astype(BF16), seq,
                                out_dtype=BF16,
                                rope=(cos_tab, sin_tab, q_norm_g[j], ATT_HQ * HEAD // 512, 1))
            branches = [dilated_attention_branch(q, kv, gi, bsz, seq) for gi in range(N_GROUPS)]
            xf = merge_matmul_gated_residual([b[0] for b in branches], [b[1] for b in branches],
                                             attn_w_o[j].astype(BF16), xf, gt1, seq)
        xf = mlp_sublayer(xf, norm_g[layer, 1], sh2, sc2, gt2, mlp_w1[layer].astype(BF16),
                          mlp_w2[layer].astype(BF16), seq)
    return xf.reshape(bsz, seq, d)
```

```python
import functools

import numpy as np
import jax
import jax.numpy as jnp
from jax import lax
from jax.experimental import pallas as pl
from jax.experimental.pallas import tpu as pltpu

F32 = jnp.float32
BF16 = jnp.bfloat16

NORM_EPS = 1e-6
HEAD = 128
GDN_NK = 16
GDN_NV = 32
GDN_CONV_K = 4
CHUNK = 64
GDN_QK = GDN_NK * HEAD
GDN_V = GDN_NV * HEAD
GDN_CONV = 2 * GDN_QK + GDN_V
GDN_BA_COL = GDN_CONV + GDN_V
GDN_PROJ_PAD = GDN_BA_COL + 512
ATT_HQ = 16
ATT_HKV = 4
ATT_G = ATT_HQ // ATT_HKV
DILATIONS = (1, 4, 16)
SPAN = 128
N_GROUPS = 3
ROPE_THETA = 500000.0
ROT_DIM = HEAD // 4
NEG_BIG = -1e30

VMEM_LIMIT = 56 * 1024 * 1024


def _cparams(*sem):
    return pltpu.CompilerParams(dimension_semantics=sem, vmem_limit_bytes=VMEM_LIMIT)


def _silu(v):
    return v * jax.nn.sigmoid(v)


def _ada_kernel(c_ref, w_ref, b_ref, o_ref):
    c = c_ref[...]
    ca = _silu(c).astype(BF16)
    w = w_ref[0].astype(BF16)
    o_ref[0] = jnp.dot(ca, w, preferred_element_type=F32) + b_ref[0]


def ada_modulation(c, w, b, tn=1024):
    nl, d, n = w.shape
    bsz = c.shape[0]
    return pl.pallas_call(
        _ada_kernel,
        grid=(nl, n // tn),
        in_specs=[
            pl.BlockSpec((bsz, d), lambda l, j: (0, 0)),
            pl.BlockSpec((1, d, tn), lambda l, j: (l, 0, j)),
            pl.BlockSpec((1, 1, tn), lambda l, j: (l, 0, j)),
        ],
        out_specs=pl.BlockSpec((1, bsz, tn), lambda l, j: (l, 0, j)),
        out_shape=jax.ShapeDtypeStruct((nl, bsz, n), F32),
        compiler_params=_cparams("parallel", "parallel"),
        name="ada_modulation",
    )(c, w, b.reshape(nl, 1, n))


ROT_HALF = ROT_DIM // 2
HEAD_LANE_ORDER = np.concatenate([np.arange(0, ROT_HALF), np.arange(ROT_DIM, HEAD // 2 + ROT_HALF),
                                  np.arange(ROT_HALF, ROT_DIM),
                                  np.arange(HEAD // 2 + ROT_HALF, HEAD)])


def _permute_head_lanes(a):
    lead = a.shape[:-1]
    a = a.reshape(lead + (a.shape[-1] // HEAD, HEAD))
    parts = [a[..., 0:ROT_HALF], a[..., ROT_DIM:HEAD // 2 + ROT_HALF], a[..., ROT_HALF:ROT_DIM],
             a[..., HEAD // 2 + ROT_HALF:]]
    return jnp.concatenate(parts, axis=-1).reshape(lead + (-1,))


def _rope_kernel(pos_ref, inv_ref, cos_ref, sin_ref):
    ang = pos_ref[...].astype(F32) * inv_ref[...]
    lane = lax.broadcasted_iota(jnp.int32, ang.shape, 1)
    c = jnp.cos(ang)
    s = jnp.sin(ang)
    first = lane < ROT_HALF
    second = (lane >= HEAD // 2) & (lane < HEAD // 2 + ROT_HALF)
    cos_ref[...] = jnp.where(first | second, c, 1.0)
    sin_ref[...] = jnp.where(first, -s, jnp.where(second, s, 0.0))


def rope_tables(positions, ts=2048):
    m = positions.size
    inv = ROPE_THETA ** (-np.arange(0, ROT_DIM, 2, dtype=np.float32) / ROT_DIM)
    inv_row = np.zeros((1, HEAD), np.float32)
    inv_row[0, :ROT_HALF] = inv
    inv_row[0, HEAD // 2:HEAD // 2 + ROT_HALF] = inv
    ts = min(ts, m)
    return pl.pallas_call(
        _rope_kernel,
        grid=(m // ts,),
        in_specs=[pl.BlockSpec((ts, 1), lambda i: (i, 0)),
                  pl.BlockSpec((1, HEAD), lambda i: (0, 0))],
        out_specs=[pl.BlockSpec((ts, HEAD), lambda i: (i, 0))] * 2,
        out_shape=[jax.ShapeDtypeStruct((m, HEAD), F32)] * 2,
        compiler_params=_cparams("parallel"),
        name="rope_tables",
    )(positions.reshape(m, 1), jnp.asarray(inv_row))


def _head_norm_rope(t, gain, cos_t, sin_t):
    y = t * lax.rsqrt(jnp.mean(t * t, axis=-1, keepdims=True) + NORM_EPS) * gain
    return y * cos_t + pltpu.roll(y, HEAD // 2, 1) * sin_t


def _norm_modulate(x, gain, shift, scale):
    y = x * lax.rsqrt(jnp.mean(x * x, axis=-1, keepdims=True) + NORM_EPS) * gain
    return y * (1.0 + scale) + shift


def _nmm_kernel(x_ref, sh_ref, sc_ref, g_ref, w_ref, o_ref, h_ref):
    @pl.when(pl.program_id(1) == 0)
    def _():
        h_ref[...] = _norm_modulate(x_ref[...], g_ref[...], sh_ref[0], sc_ref[0]).astype(BF16)

    o_ref[...] = jnp.dot(h_ref[...], w_ref[...], preferred_element_type=F32).astype(o_ref.dtype)


def norm_mod_matmul(x, gain, shift, scale, w, seq, *, out_dtype, tm=1024, tn=512):
    m, d = x.shape
    n = w.shape[1]
    bsz = shift.shape[0]
    tpb = seq // tm
    return pl.pallas_call(
        _nmm_kernel,
        grid=(m // tm, n // tn),
        in_specs=[
            pl.BlockSpec((tm, d), lambda i, j: (i, 0)),
            pl.BlockSpec((1, 1, d), lambda i, j: (i // tpb, 0, 0)),
            pl.BlockSpec((1, 1, d), lambda i, j: (i // tpb, 0, 0)),
            pl.BlockSpec((1, d), lambda i, j: (0, 0)),
            pl.BlockSpec((d, tn), lambda i, j: (0, j)),
        ],
        out_specs=pl.BlockSpec((tm, tn), lambda i, j: (i, j)),
        out_shape=jax.ShapeDtypeStruct((m, n), out_dtype),
        scratch_shapes=[pltpu.VMEM((tm, d), BF16)],
        compiler_params=_cparams("parallel", "arbitrary"),
        name="norm_mod_matmul",
    )(x, shift.reshape(bsz, 1, d), scale.reshape(bsz, 1, d), gain.reshape(1, d), w)


def _dilated_rows(r, n, dil):
    return pl.ds(r, n, stride=dil) if dil > 1 else pl.ds(0, n)


def _branch_proj_kernel(x_ref, sh_ref, sc_ref, g_ref, w_ref, cos_ref, sin_ref, hg_ref, *rest,
                        segments):
    n_out = len(segments) * N_GROUPS
    outs = rest[:n_out]
    h_ref, acc_ref = rest[n_out:]
    grp = pl.program_id(1)
    tm = x_ref.shape[0]

    @pl.when(grp == 0)
    def _():
        h_ref[...] = _norm_modulate(x_ref[...], g_ref[...], sh_ref[0], sc_ref[0]).astype(BF16)

    acc = jnp.dot(h_ref[...], w_ref[...], preferred_element_type=F32)
    gain = hg_ref[0]
    cos_t = cos_ref[...]
    sin_t = sin_ref[...]
    hh = 0
    for heads, roped in segments:
        for _ in range(heads):
            t = acc[:, hh * HEAD:(hh + 1) * HEAD]
            acc_ref[hh] = _head_norm_rope(t, gain, cos_t, sin_t) if roped else t
            hh += 1

    def write_branch(gi):
        dil = DILATIONS[gi]
        n = tm // dil
        for r in range(dil):
            rows = _dilated_rows(r, n, dil)
            hh = 0
            for si, (heads, _) in enumerate(segments):
                o_ref = outs[gi * len(segments) + si]
                for hs in range(heads):
                    col = (r * heads + hs) * HEAD
                    o_ref[0, :, col:col + HEAD] = acc_ref[hh, rows, :].astype(o_ref.dtype)
                    hh += 1

    for gi in range(N_GROUPS):
        pl.when(grp == gi)(functools.partial(write_branch, gi))


def branch_projection(x, gain, shift, scale, w, head_gain, cos_tab, sin_tab, segments, bsz, seq,
                      *, tm=512):
    m, d = x.shape
    cgrp = w.shape[1] // N_GROUPS
    nheads = cgrp // HEAD
    tpb = seq // tm
    out_specs, out_shapes = [], []
    for gi in range(N_GROUPS):
        dil = DILATIONS[gi]
        for heads, _ in segments:
            width = dil * heads * HEAD
            out_specs.append(pl.BlockSpec((1, tm // dil, width),
                                          lambda i, g: (i // tpb, i % tpb, 0)))
            out_shapes.append(jax.ShapeDtypeStruct((bsz, seq // dil, width), BF16))
    return pl.pallas_call(
        functools.partial(_branch_proj_kernel, segments=tuple(segments)),
        grid=(m // tm, N_GROUPS),
        in_specs=[
            pl.BlockSpec((tm, d), lambda i, g: (i, 0)),
            pl.BlockSpec((1, 1, d), lambda i, g: (i // tpb, 0, 0)),
            pl.BlockSpec((1, 1, d), lambda i, g: (i // tpb, 0, 0)),
            pl.BlockSpec((1, d), lambda i, g: (0, 0)),
            pl.BlockSpec((d, cgrp), lambda i, g: (0, g)),
            pl.BlockSpec((tm, HEAD), lambda i, g: (i, 0)),
            pl.BlockSpec((tm, HEAD), lambda i, g: (i, 0)),
            pl.BlockSpec((1, 1, HEAD), lambda i, g: (g, 0, 0)),
        ],
        out_specs=out_specs,
        out_shape=out_shapes,
        scratch_shapes=[pltpu.VMEM((tm, d), BF16), pltpu.VMEM((nheads, tm, HEAD), F32)],
        compiler_params=_cparams("parallel", "arbitrary"),
        name="branch_projection",
    )(x, shift.reshape(bsz, 1, d), scale.reshape(bsz, 1, d), gain.reshape(1, d), w,
      cos_tab, sin_tab, head_gain.reshape(N_GROUPS, 1, HEAD))


def _mm_res_kernel(a_ref, w_ref, x_ref, gt_ref, o_ref):
    y = jnp.dot(a_ref[...], w_ref[...], preferred_element_type=F32)
    o_ref[...] = x_ref[...] + gt_ref[0] * y


def matmul_gated_residual(a, w, x, gate, seq, *, tm=1024, tn=512):
    m, k = a.shape
    d = w.shape[1]
    bsz = gate.shape[0]
    tpb = seq // tm
    return pl.pallas_call(
        _mm_res_kernel,
        grid=(m // tm, d // tn),
        in_specs=[
            pl.BlockSpec((tm, k), lambda i, j: (i, 0)),
            pl.BlockSpec((k, tn), lambda i, j: (0, j)),
            pl.BlockSpec((tm, tn), lambda i, j: (i, j)),
            pl.BlockSpec((1, 1, tn), lambda i, j: (i // tpb, 0, j)),
        ],
        out_specs=pl.BlockSpec((tm, tn), lambda i, j: (i, j)),
        out_shape=jax.ShapeDtypeStruct((m, d), F32),
        compiler_params=_cparams("parallel", "parallel"),
        name="matmul_gated_residual",
    )(a, w, x, gate.reshape(bsz, 1, d))


def _merge_mm_res_kernel(o0_ref, o1_ref, o2_ref, l0_ref, l1_ref, l2_ref, w_ref, x_ref, gt_ref,
                         o_ref, a_ref, o3_ref, l3_ref):
    j = pl.program_id(1)
    tm = x_ref.shape[0]
    o_refs = (o0_ref, o1_ref, o2_ref)
    l_refs = (l0_ref, l1_ref, l2_ref)

    @pl.when(j == 0)
    def _():
        for gi, dil in enumerate(DILATIONS):
            n = tm // dil
            for r in range(dil):
                l3_ref[gi, _dilated_rows(r, n, dil), :] = l_refs[gi][0, :, r * HEAD:(r + 1) * HEAD]
        l0, l1, l2 = l3_ref[0], l3_ref[1], l3_ref[2]
        mx = jnp.maximum(jnp.maximum(l0, l1), l2)
        e0 = jnp.exp(l0 - mx)
        e1 = jnp.exp(l1 - mx)
        e2 = jnp.exp(l2 - mx)
        den = e0 + e1 + e2
        l3_ref[0] = e0 / den
        l3_ref[1] = e1 / den
        l3_ref[2] = e2 / den
        for gi, dil in enumerate(DILATIONS):
            n = tm // dil
            for r in range(dil):
                rows = _dilated_rows(r, n, dil)
                for hq in range(ATT_HQ):
                    col = (r * ATT_HQ + hq) * HEAD
                    o3_ref[gi, hq, rows, :] = o_refs[gi][0, :, col:col + HEAD].astype(F32)
        for hq in range(ATT_HQ):
            merged = l3_ref[0][:, hq:hq + 1] * o3_ref[0, hq]
            for gi in range(1, N_GROUPS):
                merged = merged + l3_ref[gi][:, hq:hq + 1] * o3_ref[gi, hq]
            a_ref[:, hq * HEAD:(hq + 1) * HEAD] = merged.astype(BF16)

    y = jnp.dot(a_ref[...], w_ref[...], preferred_element_type=F32)
    o_ref[...] = x_ref[...] + gt_ref[0] * y


def merge_matmul_gated_residual(outs, lses, w, x, gate, seq, *, tm=512, tn=512):
    m, d = x.shape
    k = w.shape[0]
    bsz = gate.shape[0]
    tpb = seq // tm
    o_specs = [pl.BlockSpec((1, tm // dil, dil * k), lambda i, j: (i // tpb, i % tpb, 0))
               for dil in DILATIONS]
    l_specs = [pl.BlockSpec((1, tm // dil, dil * HEAD), lambda i, j: (i // tpb, i % tpb, 0))
               for dil in DILATIONS]
    return pl.pallas_call(
        _merge_mm_res_kernel,
        grid=(m // tm, d // tn),
        in_specs=o_specs + l_specs + [
            pl.BlockSpec((k, tn), lambda i, j: (0, j)),
            pl.BlockSpec((tm, tn), lambda i, j: (i, j)),
            pl.BlockSpec((1, 1, tn), lambda i, j: (i // tpb, 0, j))],
        out_specs=pl.BlockSpec((tm, tn), lambda i, j: (i, j)),
        out_shape=jax.ShapeDtypeStruct((m, d), F32),
        scratch_shapes=[pltpu.VMEM((tm, k), BF16), pltpu.VMEM((N_GROUPS, ATT_HQ, tm, HEAD), F32),
                        pltpu.VMEM((N_GROUPS, tm, HEAD), F32)],
        compiler_params=_cparams("parallel", "arbitrary"),
        name="merge_matmul_gated_residual",
    )(*outs, *lses, w, x, gate.reshape(bsz, 1, d))


def _mlp_kernel(x_ref, sh_ref, sc_ref, gt_ref, g_ref, w1_ref, w2_ref, o_ref, h_ref, acc_ref):
    f = pl.program_id(1)

    @pl.when(f == 0)
    def _():
        h_ref[...] = _norm_modulate(x_ref[...], g_ref[...], sh_ref[0], sc_ref[0]).astype(BF16)

    a = jnp.dot(h_ref[...], w1_ref[...], preferred_element_type=F32)
    a = jnp.square(jnp.maximum(a, 0.0)).astype(BF16)
    part = jnp.dot(a, w2_ref[...], preferred_element_type=F32)

    @pl.when(f == 0)
    def _():
        acc_ref[...] = part

    @pl.when(f > 0)
    def _():
        acc_ref[...] += part

    @pl.when(f == pl.num_programs(1) - 1)
    def _():
        o_ref[...] = x_ref[...] + gt_ref[0] * acc_ref[...]


def mlp_sublayer(x, gain, shift, scale, gate, w1, w2, seq, *, tm=512, tf=1024):
    m, d = x.shape
    dff = w1.shape[1]
    bsz = gate.shape[0]
    tpb = seq // tm
    vec = pl.BlockSpec((1, 1, d), lambda i, f: (i // tpb, 0, 0))
    return pl.pallas_call(
        _mlp_kernel,
        grid=(m // tm, dff // tf),
        in_specs=[
            pl.BlockSpec((tm, d), lambda i, f: (i, 0)),
            vec, vec, vec,
            pl.BlockSpec((1, d), lambda i, f: (0, 0)),
            pl.BlockSpec((d, tf), lambda i, f: (0, f)),
            pl.BlockSpec((tf, d), lambda i, f: (f, 0)),
        ],
        out_specs=pl.BlockSpec((tm, d), lambda i, f: (i, 0)),
        out_shape=jax.ShapeDtypeStruct((m, d), F32),
        scratch_shapes=[pltpu.VMEM((tm, d), BF16), pltpu.VMEM((tm, d), F32)],
        compiler_params=_cparams("parallel", "arbitrary"),
        name="mlp_sublayer",
    )(x, shift.reshape(bsz, 1, d), scale.reshape(bsz, 1, d), gate.reshape(bsz, 1, d),
      gain.reshape(1, d), w1, w2)


def _softplus(v):
    return jnp.maximum(v, 0.0) + jnp.log1p(jnp.exp(-jnp.abs(v)))


def _gates_kernel(ba_ref, alog_ref, dtb_ref, tril_ref, g_ref, gt_ref):
    ba = ba_ref[...]
    lane = lax.broadcasted_iota(jnp.int32, ba.shape, 1)
    is_alpha = (lane >= GDN_NV) & (lane < 2 * GDN_NV)
    g = jnp.where(is_alpha, -jnp.exp(alog_ref[...]) * _softplus(ba + dtb_ref[...]), 0.0)
    gc = jnp.dot(tril_ref[...], g, preferred_element_type=F32, precision=lax.Precision.HIGHEST)
    out = jnp.where(lane < GDN_NV, jax.nn.sigmoid(ba), gc)
    g_ref[...] = out
    gt_ref[...] = out.T


def gdn_gates(proj, a_log, dt_bias, tg=512):
    m = proj.shape[0]
    tg = min(tg, m)
    alog_row = jnp.zeros((1, HEAD), F32).at[0, GDN_NV:2 * GDN_NV].set(a_log)
    dtb_row = jnp.zeros((1, HEAD), F32).at[0, GDN_NV:2 * GDN_NV].set(dt_bias)
    idx = np.arange(tg)
    tril = ((idx[:, None] >= idx[None, :]) & (idx[:, None] // CHUNK == idx[None, :] // CHUNK))
    return pl.pallas_call(
        _gates_kernel,
        grid=(m // tg,),
        in_specs=[pl.BlockSpec((tg, HEAD), lambda i: (i, GDN_BA_COL // HEAD)),
                  pl.BlockSpec((1, HEAD), lambda i: (0, 0)),
                  pl.BlockSpec((1, HEAD), lambda i: (0, 0)),
                  pl.BlockSpec((tg, tg), lambda i: (0, 0))],
        out_specs=[pl.BlockSpec((tg, HEAD), lambda i: (i, 0)),
                   pl.BlockSpec((HEAD, tg), lambda i: (0, i))],
        out_shape=[jax.ShapeDtypeStruct((m, HEAD), F32), jax.ShapeDtypeStruct((HEAD, m), F32)],
        compiler_params=_cparams("parallel"),
        name="gdn_gates",
    )(proj, alog_row, dtb_row, jnp.asarray(tril.astype(np.float32)))


def _dot_t(a, b):
    return lax.dot_general(a, b, (((1,), (1,)), ((), ())), preferred_element_type=F32)


def _tdot(a, b):
    return lax.dot_general(a, b, (((0,), (0,)), ((), ())), preferred_element_type=F32)


def _bmm(a, b):
    return jnp.einsum("bij,bjk->bik", a.astype(BF16), b.astype(BF16),
                      preferred_element_type=F32)


def _bmm_t(a, b):
    return jnp.einsum("bik,bjk->bij", a.astype(BF16), b.astype(BF16),
                      preferred_element_type=F32)


def _unit_lower_inverse(lmat):
    i = lax.broadcasted_iota(jnp.int32, (CHUNK, CHUNK), 0)
    j = lax.broadcasted_iota(jnp.int32, (CHUNK, CHUNK), 1)

    def sub_blocks(s):
        return (((i >> s) & 1) == 1) & ((j >> s) == (i >> s) - 1)

    inv = jnp.where(i == j, 1.0, 0.0) - jnp.where(sub_blocks(0), lmat, 0.0)
    s = 1
    while (1 << s) < CHUNK:
        inv = inv - _bmm(_bmm(inv, jnp.where(sub_blocks(s), lmat, 0.0)), inv)
        s += 1
    return inv


def _gdn_kernel(q_ref, k_ref, v_ref, z_ref, wq_ref, wk_ref, wv_ref, g_ref, gt_ref, og_ref,
                o_ref, qbuf, kbuf, vbuf, state_ref, *, tt, hg):
    hgi = pl.program_id(1)
    t = pl.program_id(2)
    nc = tt // CHUNK
    nh = 2 * hg

    @pl.when(t == 0)
    def _():
        state_ref[...] = jnp.zeros_like(state_ref)
        qbuf[pl.ds(tt, 8), :] = jnp.zeros((8, hg * HEAD), F32)
        kbuf[pl.ds(tt, 8), :] = jnp.zeros((8, hg * HEAD), F32)
        vbuf[pl.ds(tt, 8), :] = jnp.zeros((8, nh * HEAD), F32)

    def conv_silu(x_ref, buf, w_ref):
        buf[pl.ds(0, 8), :] = buf[pl.ds(tt, 8), :]
        buf[pl.ds(8, tt), :] = x_ref[0]
        acc = w_ref[GDN_CONV_K - 1:GDN_CONV_K, :] * buf[pl.ds(8, tt), :]
        for j in range(1, GDN_CONV_K):
            tap = GDN_CONV_K - 1 - j
            acc = acc + w_ref[tap:tap + 1, :] * buf[pl.ds(8 - j, tt), :]
        return _silu(acc)

    def l2n(v):
        return v * lax.rsqrt(jnp.sum(v * v, axis=-1, keepdims=True) + NORM_EPS)

    def chunked(v):
        return v.reshape(nc, CHUNK, HEAD)

    qa = conv_silu(q_ref, qbuf, wq_ref)
    ka = conv_silu(k_ref, kbuf, wk_ref)
    va = conv_silu(v_ref, vbuf, wv_ref)

    gates = g_ref[0]
    lane = lax.broadcasted_iota(jnp.int32, gates.shape, 1)
    row_i = lax.broadcasted_iota(jnp.int32, (CHUNK, CHUNK), 0)
    col_j = lax.broadcasted_iota(jnp.int32, (CHUNK, CHUNK), 1)
    causal = row_i >= col_j
    strict = row_i > col_j

    lmats, rhss, qgs, kgs, pmats, elasts = [], [], [], [], [], []
    for khl in range(hg):
        hs = slice(khl * HEAD, (khl + 1) * HEAD)
        qc = chunked(l2n(qa[:, hs]) * (HEAD ** -0.5))
        kc = chunked(l2n(ka[:, hs]))
        kk = _bmm_t(kc, kc)
        qk = _bmm_t(qc, kc)
        for e in range(2):
            hl = 2 * khl + e
            hv = hgi * nh + hl
            beta_col = jnp.sum(jnp.where(lane == hv, gates, 0.0), axis=-1, keepdims=True)
            gc_col = jnp.sum(jnp.where(lane == hv + GDN_NV, gates, 0.0), axis=-1, keepdims=True)
            gc_row = gt_ref[pl.ds(hv + GDN_NV, 1), :]
            bi = beta_col.reshape(nc, CHUNK, 1)
            gci = gc_col.reshape(nc, CHUNK, 1)
            gcj = jnp.stack([gc_row[:, c * CHUNK:(c + 1) * CHUNK] for c in range(nc)], axis=0)
            dec = jnp.exp(jnp.where(causal, gci - gcj, NEG_BIG))
            egc = jnp.exp(gci)
            g_last = gci[:, CHUNK - 1:CHUNK, :]
            vc = chunked(va[:, hl * HEAD:(hl + 1) * HEAD])
            lmats.append(jnp.where(strict, kk * bi * dec, 0.0))
            rhss.append(jnp.concatenate([vc * bi, kc * (bi * egc)], axis=-1))
            qgs.append((qc * egc).astype(BF16))
            kgs.append((kc * jnp.exp(g_last - gci)).astype(BF16))
            pmats.append((qk * dec).astype(BF16))
            elasts.append(jnp.exp(g_last))
    inv = _unit_lower_inverse(jnp.concatenate(lmats, axis=0))
    sol = _bmm(inv, jnp.concatenate(rhss, axis=0))

    states = [state_ref[hl] for hl in range(nh)]
    o_chunks = [[] for _ in range(nh)]
    for c in range(nc):
        for hl in range(nh):
            b = hl * nc + c
            u, w = sol[b, :, :HEAD], sol[b, :, HEAD:]
            ws_qs = jnp.dot(jnp.concatenate([w.astype(BF16), qgs[hl][c]], axis=0),
                            states[hl].astype(BF16), preferred_element_type=F32)
            vnb = (u - ws_qs[:CHUNK]).astype(BF16)
            o_chunks[hl].append(
                ws_qs[CHUNK:] + jnp.dot(pmats[hl][c], vnb, preferred_element_type=F32))
            states[hl] = states[hl] * elasts[hl][c] + _tdot(kgs[hl][c], vnb)
    for hl in range(nh):
        state_ref[hl] = states[hl]
        o = jnp.concatenate(o_chunks[hl], axis=0)
        z = z_ref[0, :, hl * HEAD:(hl + 1) * HEAD]
        y = o * lax.rsqrt(jnp.mean(o * o, axis=-1, keepdims=True) + NORM_EPS) * og_ref[...]
        o_ref[0, :, hl * HEAD:(hl + 1) * HEAD] = (y * _silu(z)).astype(BF16)


def gdn_core(proj, conv_w, gates, gates_t, onorm_g, bsz, seq, *, tt=256, hg=4):
    tt = min(tt, seq)
    nt = seq // tt
    nh = 2 * hg
    proj3 = proj.reshape(bsz, seq, proj.shape[1])
    qw = hg * HEAD
    vw = nh * HEAD
    kq = GDN_QK // qw
    kv = 2 * GDN_QK // vw
    kz = GDN_CONV // vw
    out = pl.pallas_call(
        functools.partial(_gdn_kernel, tt=tt, hg=hg),
        grid=(bsz, GDN_NK // hg, nt),
        in_specs=[
            pl.BlockSpec((1, tt, qw), lambda b, h, t: (b, t, h)),
            pl.BlockSpec((1, tt, qw), lambda b, h, t: (b, t, kq + h)),
            pl.BlockSpec((1, tt, vw), lambda b, h, t: (b, t, kv + h)),
            pl.BlockSpec((1, tt, vw), lambda b, h, t: (b, t, kz + h)),
            pl.BlockSpec((GDN_CONV_K, qw), lambda b, h, t: (0, h)),
            pl.BlockSpec((GDN_CONV_K, qw), lambda b, h, t: (0, kq + h)),
            pl.BlockSpec((GDN_CONV_K, vw), lambda b, h, t: (0, kv + h)),
            pl.BlockSpec((1, tt, HEAD), lambda b, h, t: (b, t, 0)),
            pl.BlockSpec((HEAD, tt), lambda b, h, t: (0, b * nt + t)),
            pl.BlockSpec((1, HEAD), lambda b, h, t: (0, 0)),
        ],
        out_specs=pl.BlockSpec((1, tt, vw), lambda b, h, t: (b, t, h)),
        out_shape=jax.ShapeDtypeStruct((bsz, seq, GDN_V), BF16),
        scratch_shapes=[pltpu.VMEM((tt + 8, qw), F32), pltpu.VMEM((tt + 8, qw), F32),
                        pltpu.VMEM((tt + 8, vw), F32), pltpu.VMEM((nh, HEAD, HEAD), F32)],
        compiler_params=_cparams("parallel", "parallel", "arbitrary"),
        name="gdn_core",
    )(proj3, proj3, proj3, proj3, conv_w, conv_w, conv_w,
      gates.reshape(bsz, seq, HEAD), gates_t, onorm_g.reshape(1, HEAD))
    return out.reshape(bsz * seq, GDN_V)


def _attn_kernel(q_ref, kp_ref, kc_ref, vp_ref, vc_ref, o_ref, lse_ref):
    lb = pl.program_id(2)
    rows = ATT_G * SPAN
    qi = lax.broadcasted_iota(jnp.int32, (rows, 2 * SPAN), 0) & (SPAN - 1)
    kj = lax.broadcasted_iota(jnp.int32, (rows, 2 * SPAN), 1)
    rel = qi + SPAN - kj
    mask = (rel >= 0) & (rel <= SPAN) & (kj >= jnp.where(lb > 0, 0, SPAN))
    scale = HEAD ** -0.5
    lse_ref[0] = jnp.zeros(lse_ref.shape[1:], F32)
    for h in range(ATT_HKV):
        hs = slice(h * HEAD, (h + 1) * HEAD)
        k2 = jnp.concatenate([kp_ref[0, :, hs], kc_ref[0, :, hs]], axis=0)
        v2 = jnp.concatenate([vp_ref[0, :, hs], vc_ref[0, :, hs]], axis=0)
        q4 = jnp.concatenate(
            [q_ref[0, :, (h * ATT_G + g) * HEAD:(h * ATT_G + g + 1) * HEAD] for g in range(ATT_G)],
            axis=0)
        s = jnp.where(mask, _dot_t(q4, k2) * scale, NEG_BIG)
        mx = jnp.max(s, axis=-1, keepdims=True)
        p = jnp.exp(s - mx)
        den = jnp.sum(p, axis=-1, keepdims=True)
        o4 = jnp.dot(p.astype(BF16), v2, preferred_element_type=F32) / den
        lse = mx + jnp.log(den)
        for g in range(ATT_G):
            hq = h * ATT_G + g
            o_ref[0, :, hq * HEAD:(hq + 1) * HEAD] = o4[g * SPAN:(g + 1) * SPAN].astype(o_ref.dtype)
            lse_ref[0, :, hq:hq + 1] = lse[g * SPAN:(g + 1) * SPAN]


def dilated_attention_branch(q, k, v, gi):
    dil = DILATIONS[gi]
    bsz, sub, _ = q.shape
    nb = sub // SPAN
    qd = ATT_HQ * HEAD
    kd = ATT_HKV * HEAD

    def cur(b, r, lb):
        return (b, lb, r)

    def prev(b, r, lb):
        return (b, jnp.maximum(lb - 1, 0), r)

    return pl.pallas_call(
        _attn_kernel,
        grid=(bsz, dil, nb),
        in_specs=[
            pl.BlockSpec((1, SPAN, qd), cur),
            pl.BlockSpec((1, SPAN, kd), prev),
            pl.BlockSpec((1, SPAN, kd), cur),
            pl.BlockSpec((1, SPAN, kd), prev),
            pl.BlockSpec((1, SPAN, kd), cur),
        ],
        out_specs=[pl.BlockSpec((1, SPAN, qd), cur), pl.BlockSpec((1, SPAN, HEAD), cur)],
        out_shape=[jax.ShapeDtypeStruct((bsz, sub, dil * qd), BF16),
                   jax.ShapeDtypeStruct((bsz, sub, dil * HEAD), F32)],
        compiler_params=_cparams("parallel", "parallel", "parallel"),
        name=f"dilated_attention_{gi}",
    )(q, k, k, v, v)


def kernel(x, c, positions, ada_w, ada_b, norm_g, mlp_w1, mlp_w2, gdn_w_in, gdn_conv_w, gdn_a_log,
           gdn_dt_bias, gdn_onorm_g, gdn_w_out, kv_norm_g, kv_ada_w, kv_ada_b, w_kv, k_norm_g,
           attn_w_q, q_norm_g, attn_w_o):
    bsz, seq, d = x.shape
    depth = ada_w.shape[0]
    n_a = gdn_w_in.shape[0]
    m = bsz * seq
    xf = x.reshape(m, d)

    mod = ada_modulation(c, ada_w, ada_b)
    kv_mod = ada_modulation(c, kv_ada_w[None], kv_ada_b[None])[0]
    cos_tab, sin_tab = rope_tables(positions)

    kv_cols = w_kv.reshape(d, N_GROUPS, 2, ATT_HKV * HEAD)
    w_kv_p = jnp.concatenate([_permute_head_lanes(kv_cols[:, :, 0]), kv_cols[:, :, 1]],
                             axis=-1).reshape(d, -1).astype(BF16)
    k_gain_p = _permute_head_lanes(k_norm_g)

    kv = None
    for layer in range(depth):
        sh1, sc1, gt1, sh2, sc2, gt2 = [mod[layer, :, i * d:(i + 1) * d] for i in range(6)]
        if layer < n_a:
            w_in = jnp.pad(gdn_w_in[layer], ((0, 0), (0, GDN_PROJ_PAD - gdn_w_in.shape[2])))
            proj = norm_mod_matmul(xf, norm_g[layer, 0], sh1, sc1, w_in.astype(BF16), seq,
                                   out_dtype=F32)
            gates, gates_t = gdn_gates(proj, gdn_a_log[layer], gdn_dt_bias[layer])
            o = gdn_core(proj, gdn_conv_w[layer], gates, gates_t, gdn_onorm_g[layer], bsz, seq)
            xf = matmul_gated_residual(o, gdn_w_out[layer].astype(BF16), xf, gt1, seq)
        else:
            if kv is None:
                kv = branch_projection(xf, kv_norm_g, kv_mod[:, :d], kv_mod[:, d:], w_kv_p,
                                       k_gain_p, cos_tab, sin_tab,
                                       [(ATT_HKV, True), (ATT_HKV, False)], bsz, seq)
            j = layer - n_a
            qs = branch_projection(xf, norm_g[layer, 0], sh1, sc1,
                                   _permute_head_lanes(attn_w_q[j]).astype(BF16),
                                   _permute_head_lanes(q_norm_g[j]), cos_tab, sin_tab,
                                   [(ATT_HQ, True)], bsz, seq)
            branches = [dilated_attention_branch(qs[gi], kv[2 * gi], kv[2 * gi + 1], gi)
                        for gi in range(N_GROUPS)]
            xf = merge_matmul_gated_residual([b[0] for b in branches], [b[1] for b in branches],
                                             attn_w_o[j].astype(BF16), xf, gt1, seq)
        xf = mlp_sublayer(xf, norm_g[layer, 1], sh2, sc2, gt2, mlp_w1[layer].astype(BF16),
                          mlp_w2[layer].astype(BF16), seq)
    return xf.reshape(bsz, seq, d)
```

```python
import functools

import numpy as np
import jax
import jax.numpy as jnp
from jax import lax
from jax.experimental import pallas as pl
from jax.experimental.pallas import tpu as pltpu

F32 = jnp.float32
BF16 = jnp.bfloat16

NORM_EPS = 1e-6
HEAD = 128
GDN_NK = 16
GDN_NV = 32
GDN_CONV_K = 4
CHUNK = 64
GDN_QK = GDN_NK * HEAD
GDN_V = GDN_NV * HEAD
GDN_CONV = 2 * GDN_QK + GDN_V
GDN_BA_COL = GDN_CONV + GDN_V
GDN_PROJ_PAD = GDN_BA_COL + 512
ATT_HQ = 16
ATT_HKV = 4
ATT_G = ATT_HQ // ATT_HKV
DILATIONS = (1, 4, 16)
SPAN = 128
N_GROUPS = 3
ROPE_THETA = 500000.0
ROT_DIM = HEAD // 4
NEG_BIG = -1e30

VMEM_LIMIT = 56 * 1024 * 1024


def _cparams(*sem):
    return pltpu.CompilerParams(dimension_semantics=sem, vmem_limit_bytes=VMEM_LIMIT)


def _silu(v):
    return v * jax.nn.sigmoid(v)


def _ada_kernel(c_ref, w_ref, b_ref, o_ref):
    c = c_ref[...]
    ca = _silu(c).astype(BF16)
    w = w_ref[0].astype(BF16)
    o_ref[0] = jnp.dot(ca, w, preferred_element_type=F32) + b_ref[0]


def ada_modulation(c, w, b, tn=1024):
    nl, d, n = w.shape
    bsz = c.shape[0]
    return pl.pallas_call(
        _ada_kernel,
        grid=(nl, n // tn),
        in_specs=[
            pl.BlockSpec((bsz, d), lambda l, j: (0, 0)),
            pl.BlockSpec((1, d, tn), lambda l, j: (l, 0, j)),
            pl.BlockSpec((1, 1, tn), lambda l, j: (l, 0, j)),
        ],
        out_specs=pl.BlockSpec((1, bsz, tn), lambda l, j: (l, 0, j)),
        out_shape=jax.ShapeDtypeStruct((nl, bsz, n), F32),
        compiler_params=_cparams("parallel", "parallel"),
        name="ada_modulation",
    )(c, w, b.reshape(nl, 1, n))


ROT_HALF = ROT_DIM // 2
HEAD_LANE_ORDER = np.concatenate([np.arange(0, ROT_HALF), np.arange(ROT_DIM, HEAD // 2 + ROT_HALF),
                                  np.arange(ROT_HALF, ROT_DIM),
                                  np.arange(HEAD // 2 + ROT_HALF, HEAD)])


def _permute_head_lanes(a):
    lead = a.shape[:-1]
    a = a.reshape(lead + (a.shape[-1] // HEAD, HEAD))
    parts = [a[..., 0:ROT_HALF], a[..., ROT_DIM:HEAD // 2 + ROT_HALF], a[..., ROT_HALF:ROT_DIM],
             a[..., HEAD // 2 + ROT_HALF:]]
    return jnp.concatenate(parts, axis=-1).reshape(lead + (-1,))


def _rope_kernel(pos_ref, inv_ref, cos_ref, sin_ref):
    ang = pos_ref[...].astype(F32) * inv_ref[...]
    lane = lax.broadcasted_iota(jnp.int32, ang.shape, 1)
    c = jnp.cos(ang)
    s = jnp.sin(ang)
    first = lane < ROT_HALF
    second = (lane >= HEAD // 2) & (lane < HEAD // 2 + ROT_HALF)
    cos_ref[...] = jnp.where(first | second, c, 1.0)
    sin_ref[...] = jnp.where(first, -s, jnp.where(second, s, 0.0))


def rope_tables(positions, ts=2048):
    m = positions.size
    inv = ROPE_THETA ** (-np.arange(0, ROT_DIM, 2, dtype=np.float32) / ROT_DIM)
    inv_row = np.zeros((1, HEAD), np.float32)
    inv_row[0, :ROT_HALF] = inv
    inv_row[0, HEAD // 2:HEAD // 2 + ROT_HALF] = inv
    ts = min(ts, m)
    return pl.pallas_call(
        _rope_kernel,
        grid=(m // ts,),
        in_specs=[pl.BlockSpec((ts, 1), lambda i: (i, 0)),
                  pl.BlockSpec((1, HEAD), lambda i: (0, 0))],
        out_specs=[pl.BlockSpec((ts, HEAD), lambda i: (i, 0))] * 2,
        out_shape=[jax.ShapeDtypeStruct((m, HEAD), F32)] * 2,
        compiler_params=_cparams("parallel"),
        name="rope_tables",
    )(positions.reshape(m, 1), jnp.asarray(inv_row))


def _head_norm_rope(t, gain, cos_t, sin_t):
    y = t * lax.rsqrt(jnp.mean(t * t, axis=-1, keepdims=True) + NORM_EPS) * gain
    return y * cos_t + pltpu.roll(y, HEAD // 2, 1) * sin_t


def _norm_modulate(x, gain, shift, scale):
    y = x * lax.rsqrt(jnp.mean(x * x, axis=-1, keepdims=True) + NORM_EPS) * gain
    return y * (1.0 + scale) + shift


def _nmm_kernel(x_ref, sh_ref, sc_ref, g_ref, w_ref, o_ref, h_ref):
    @pl.when(pl.program_id(1) == 0)
    def _():
        h_ref[...] = _norm_modulate(x_ref[...], g_ref[...], sh_ref[0], sc_ref[0]).astype(BF16)

    o_ref[...] = jnp.dot(h_ref[...], w_ref[...], preferred_element_type=F32).astype(o_ref.dtype)


def norm_mod_matmul(x, gain, shift, scale, w, seq, *, out_dtype, tm=1024, tn=512):
    m, d = x.shape
    n = w.shape[1]
    bsz = shift.shape[0]
    tpb = seq // tm
    return pl.pallas_call(
        _nmm_kernel,
        grid=(m // tm, n // tn),
        in_specs=[
            pl.BlockSpec((tm, d), lambda i, j: (i, 0)),
            pl.BlockSpec((1, 1, d), lambda i, j: (i // tpb, 0, 0)),
            pl.BlockSpec((1, 1, d), lambda i, j: (i // tpb, 0, 0)),
            pl.BlockSpec((1, d), lambda i, j: (0, 0)),
            pl.BlockSpec((d, tn), lambda i, j: (0, j)),
        ],
        out_specs=pl.BlockSpec((tm, tn), lambda i, j: (i, j)),
        out_shape=jax.ShapeDtypeStruct((m, n), out_dtype),
        scratch_shapes=[pltpu.VMEM((tm, d), BF16)],
        compiler_params=_cparams("parallel", "arbitrary"),
        name="norm_mod_matmul",
    )(x, shift.reshape(bsz, 1, d), scale.reshape(bsz, 1, d), gain.reshape(1, d), w)


def _dilated_rows(r, n, dil):
    return pl.ds(r, n, stride=dil) if dil > 1 else pl.ds(0, n)


def _branch_proj_kernel(x_ref, sh_ref, sc_ref, g_ref, w_ref, cos_ref, sin_ref, hg_ref, *rest,
                        segments):
    n_out = len(segments) * N_GROUPS
    outs = rest[:n_out]
    h_ref, acc_ref = rest[n_out:]
    grp = pl.program_id(1)
    tm = x_ref.shape[0]

    @pl.when(grp == 0)
    def _():
        h_ref[...] = _norm_modulate(x_ref[...], g_ref[...], sh_ref[0], sc_ref[0]).astype(BF16)

    acc = jnp.dot(h_ref[...], w_ref[...], preferred_element_type=F32)
    gain = hg_ref[0]
    cos_t = cos_ref[...]
    sin_t = sin_ref[...]
    hh = 0
    for heads, roped in segments:
        for _ in range(heads):
            t = acc[:, hh * HEAD:(hh + 1) * HEAD]
            acc_ref[hh] = _head_norm_rope(t, gain, cos_t, sin_t) if roped else t
            hh += 1

    def write_branch(gi):
        dil = DILATIONS[gi]
        n = tm // dil
        for r in range(dil):
            rows = _dilated_rows(r, n, dil)
            hh = 0
            for si, (heads, _) in enumerate(segments):
                o_ref = outs[gi * len(segments) + si]
                for hs in range(heads):
                    col = (r * heads + hs) * HEAD
                    o_ref[0, :, col:col + HEAD] = acc_ref[hh, rows, :].astype(o_ref.dtype)
                    hh += 1

    for gi in range(N_GROUPS):
        pl.when(grp == gi)(functools.partial(write_branch, gi))


def branch_projection(x, gain, shift, scale, w, head_gain, cos_tab, sin_tab, segments, bsz, seq,
                      *, tm=512):
    m, d = x.shape
    cgrp = w.shape[1] // N_GROUPS
    nheads = cgrp // HEAD
    tpb = seq // tm
    out_specs, out_shapes = [], []
    for gi in range(N_GROUPS):
        dil = DILATIONS[gi]
        for heads, _ in segments:
            width = dil * heads * HEAD
            out_specs.append(pl.BlockSpec((1, tm // dil, width),
                                          lambda i, g: (i // tpb, i % tpb, 0)))
            out_shapes.append(jax.ShapeDtypeStruct((bsz, seq // dil, width), BF16))
    return pl.pallas_call(
        functools.partial(_branch_proj_kernel, segments=tuple(segments)),
        grid=(m // tm, N_GROUPS),
        in_specs=[
            pl.BlockSpec((tm, d), lambda i, g: (i, 0)),
            pl.BlockSpec((1, 1, d), lambda i, g: (i // tpb, 0, 0)),
            pl.BlockSpec((1, 1, d), lambda i, g: (i // tpb, 0, 0)),
            pl.BlockSpec((1, d), lambda i, g: (0, 0)),
            pl.BlockSpec((d, cgrp), lambda i, g: (0, g)),
            pl.BlockSpec((tm, HEAD), lambda i, g: (i, 0)),
            pl.BlockSpec((tm, HEAD), lambda i, g: (i, 0)),
            pl.BlockSpec((1, 1, HEAD), lambda i, g: (g, 0, 0)),
        ],
        out_specs=out_specs,
        out_shape=out_shapes,
        scratch_shapes=[pltpu.VMEM((tm, d), BF16), pltpu.VMEM((nheads, tm, HEAD), F32)],
        compiler_params=_cparams("parallel", "arbitrary"),
        name="branch_projection",
    )(x, shift.reshape(bsz, 1, d), scale.reshape(bsz, 1, d), gain.reshape(1, d), w,
      cos_tab, sin_tab, head_gain.reshape(N_GROUPS, 1, HEAD))


def _mm_res_kernel(a_ref, w_ref, x_ref, gt_ref, o_ref):
    y = jnp.dot(a_ref[...], w_ref[...], preferred_element_type=F32)
    o_ref[...] = x_ref[...] + gt_ref[0] * y


def matmul_gated_residual(a, w, x, gate, seq, *, tm=1024, tn=512):
    m, k = a.shape
    d = w.shape[1]
    bsz = gate.shape[0]
    tpb = seq // tm
    return pl.pallas_call(
        _mm_res_kernel,
        grid=(m // tm, d // tn),
        in_specs=[
            pl.BlockSpec((tm, k), lambda i, j: (i, 0)),
            pl.BlockSpec((k, tn), lambda i, j: (0, j)),
            pl.BlockSpec((tm, tn), lambda i, j: (i, j)),
            pl.BlockSpec((1, 1, tn), lambda i, j: (i // tpb, 0, j)),
        ],
        out_specs=pl.BlockSpec((tm, tn), lambda i, j: (i, j)),
        out_shape=jax.ShapeDtypeStruct((m, d), F32),
        compiler_params=_cparams("parallel", "parallel"),
        name="matmul_gated_residual",
    )(a, w, x, gate.reshape(bsz, 1, d))


def _merge_mm_res_kernel(o0_ref, o1_ref, o2_ref, l0_ref, l1_ref, l2_ref, w_ref, x_ref, gt_ref,
                         o_ref, a_ref, o3_ref, l3_ref):
    j = pl.program_id(1)
    tm = x_ref.shape[0]
    o_refs = (o0_ref, o1_ref, o2_ref)
    l_refs = (l0_ref, l1_ref, l2_ref)

    @pl.when(j == 0)
    def _():
        for gi, dil in enumerate(DILATIONS):
            n = tm // dil
            for r in range(dil):
                l3_ref[gi, _dilated_rows(r, n, dil), :] = l_refs[gi][0, :, r * HEAD:(r + 1) * HEAD]
        l0, l1, l2 = l3_ref[0], l3_ref[1], l3_ref[2]
        mx = jnp.maximum(jnp.maximum(l0, l1), l2)
        e0 = jnp.exp(l0 - mx)
        e1 = jnp.exp(l1 - mx)
        e2 = jnp.exp(l2 - mx)
        den = e0 + e1 + e2
        l3_ref[0] = e0 / den
        l3_ref[1] = e1 / den
        l3_ref[2] = e2 / den
        for gi, dil in enumerate(DILATIONS):
            n = tm // dil
            for r in range(dil):
                rows = _dilated_rows(r, n, dil)
                for hq in range(ATT_HQ):
                    col = (r * ATT_HQ + hq) * HEAD
                    o3_ref[gi, hq, rows, :] = o_refs[gi][0, :, col:col + HEAD].astype(F32)
        for hq in range(ATT_HQ):
            merged = l3_ref[0][:, hq:hq + 1] * o3_ref[0, hq]
            for gi in range(1, N_GROUPS):
                merged = merged + l3_ref[gi][:, hq:hq + 1] * o3_ref[gi, hq]
            a_ref[:, hq * HEAD:(hq + 1) * HEAD] = merged.astype(BF16)

    y = jnp.dot(a_ref[...], w_ref[...], preferred_element_type=F32)
    o_ref[...] = x_ref[...] + gt_ref[0] * y


def merge_matmul_gated_residual(outs, lses, w, x, gate, seq, *, tm=512, tn=512):
    m, d = x.shape
    k = w.shape[0]
    bsz = gate.shape[0]
    tpb = seq // tm
    o_specs = [pl.BlockSpec((1, tm // dil, dil * k), lambda i, j: (i // tpb, i % tpb, 0))
               for dil in DILATIONS]
    l_specs = [pl.BlockSpec((1, tm // dil, dil * HEAD), lambda i, j: (i // tpb, i % tpb, 0))
               for dil in DILATIONS]
    return pl.pallas_call(
        _merge_mm_res_kernel,
        grid=(m // tm, d // tn),
        in_specs=o_specs + l_specs + [
            pl.BlockSpec((k, tn), lambda i, j: (0, j)),
            pl.BlockSpec((tm, tn), lambda i, j: (i, j)),
            pl.BlockSpec((1, 1, tn), lambda i, j: (i // tpb, 0, j))],
        out_specs=pl.BlockSpec((tm, tn), lambda i, j: (i, j)),
        out_shape=jax.ShapeDtypeStruct((m, d), F32),
        scratch_shapes=[pltpu.VMEM((tm, k), BF16), pltpu.VMEM((N_GROUPS, ATT_HQ, tm, HEAD), F32),
                        pltpu.VMEM((N_GROUPS, tm, HEAD), F32)],
        compiler_params=_cparams("parallel", "arbitrary"),
        name="merge_matmul_gated_residual",
    )(*outs, *lses, w, x, gate.reshape(bsz, 1, d))


def _mlp_kernel(x_ref, sh_ref, sc_ref, gt_ref, g_ref, w1_ref, w2_ref, o_ref, h_ref, acc_ref,
                a_ref, *, nf):
    f = pl.program_id(1)

    def up():
        a = jnp.dot(h_ref[...], w1_ref[...], preferred_element_type=F32)
        return jnp.square(jnp.maximum(a, 0.0)).astype(BF16)

    def down(slot):
        acc_ref[...] += jnp.dot(a_ref[slot], w2_ref[...], preferred_element_type=F32)

    @pl.when(f == 0)
    def _():
        h_ref[...] = _norm_modulate(x_ref[...], g_ref[...], sh_ref[0], sc_ref[0]).astype(BF16)
        acc_ref[...] = jnp.zeros_like(acc_ref)
        a_ref[0] = up()

    for slot in range(2):
        @pl.when((f > 0) & (f < nf) & (f % 2 == slot))
        def _(slot=slot):
            a_new = up()
            down(1 - slot)
            a_ref[slot] = a_new

    @pl.when(f == nf)
    def _():
        down((nf - 1) % 2)
        o_ref[...] = x_ref[...] + gt_ref[0] * acc_ref[...]


def mlp_sublayer(x, gain, shift, scale, gate, w1, w2, seq, *, tm=512, tf=1024):
    m, d = x.shape
    dff = w1.shape[1]
    nf = dff // tf
    bsz = gate.shape[0]
    tpb = seq // tm
    vec = pl.BlockSpec((1, 1, d), lambda i, f: (i // tpb, 0, 0))
    return pl.pallas_call(
        functools.partial(_mlp_kernel, nf=nf),
        grid=(m // tm, nf + 1),
        in_specs=[
            pl.BlockSpec((tm, d), lambda i, f: (i, 0)),
            vec, vec, vec,
            pl.BlockSpec((1, d), lambda i, f: (0, 0)),
            pl.BlockSpec((d, tf), lambda i, f: (0, jnp.minimum(f, nf - 1))),
            pl.BlockSpec((tf, d), lambda i, f: (jnp.maximum(f - 1, 0), 0)),
        ],
        out_specs=pl.BlockSpec((tm, d), lambda i, f: (i, 0)),
        out_shape=jax.ShapeDtypeStruct((m, d), F32),
        scratch_shapes=[pltpu.VMEM((tm, d), BF16), pltpu.VMEM((tm, d), F32),
                        pltpu.VMEM((2, tm, tf), BF16)],
        compiler_params=_cparams("parallel", "arbitrary"),
        name="mlp_sublayer",
    )(x, shift.reshape(bsz, 1, d), scale.reshape(bsz, 1, d), gate.reshape(bsz, 1, d),
      gain.reshape(1, d), w1, w2)


def _softplus(v):
    return jnp.maximum(v, 0.0) + jnp.log1p(jnp.exp(-jnp.abs(v)))


def _gates_kernel(ba_ref, alog_ref, dtb_ref, tril_ref, g_ref, gt_ref):
    ba = ba_ref[...]
    lane = lax.broadcasted_iota(jnp.int32, ba.shape, 1)
    is_alpha = (lane >= GDN_NV) & (lane < 2 * GDN_NV)
    g = jnp.where(is_alpha, -jnp.exp(alog_ref[...]) * _softplus(ba + dtb_ref[...]), 0.0)
    gc = jnp.dot(tril_ref[...], g, preferred_element_type=F32, precision=lax.Precision.HIGHEST)
    out = jnp.where(lane < GDN_NV, jax.nn.sigmoid(ba), gc)
    g_ref[...] = out
    gt_ref[...] = out.T


def gdn_gates(proj, a_log, dt_bias, tg=512):
    m = proj.shape[0]
    tg = min(tg, m)
    alog_row = jnp.zeros((1, HEAD), F32).at[0, GDN_NV:2 * GDN_NV].set(a_log)
    dtb_row = jnp.zeros((1, HEAD), F32).at[0, GDN_NV:2 * GDN_NV].set(dt_bias)
    idx = np.arange(tg)
    tril = ((idx[:, None] >= idx[None, :]) & (idx[:, None] // CHUNK == idx[None, :] // CHUNK))
    return pl.pallas_call(
        _gates_kernel,
        grid=(m // tg,),
        in_specs=[pl.BlockSpec((tg, HEAD), lambda i: (i, GDN_BA_COL // HEAD)),
                  pl.BlockSpec((1, HEAD), lambda i: (0, 0)),
                  pl.BlockSpec((1, HEAD), lambda i: (0, 0)),
                  pl.BlockSpec((tg, tg), lambda i: (0, 0))],
        out_specs=[pl.BlockSpec((tg, HEAD), lambda i: (i, 0)),
                   pl.BlockSpec((HEAD, tg), lambda i: (0, i))],
        out_shape=[jax.ShapeDtypeStruct((m, HEAD), F32), jax.ShapeDtypeStruct((HEAD, m), F32)],
        compiler_params=_cparams("parallel"),
        name="gdn_gates",
    )(proj, alog_row, dtb_row, jnp.asarray(tril.astype(np.float32)))


def _dot_t(a, b):
    return lax.dot_general(a, b, (((1,), (1,)), ((), ())), preferred_element_type=F32)


def _tdot(a, b):
    return lax.dot_general(a, b, (((0,), (0,)), ((), ())), preferred_element_type=F32)


def _bmm(a, b):
    return jnp.einsum("bij,bjk->bik", a.astype(BF16), b.astype(BF16),
                      preferred_element_type=F32)


def _bmm_t(a, b):
    return jnp.einsum("bik,bjk->bij", a.astype(BF16), b.astype(BF16),
                      preferred_element_type=F32)


def _pair_blockdiag(m2):
    first = lax.broadcasted_iota(jnp.int32, m2.shape[1:], 1) < CHUNK
    return jnp.concatenate([jnp.where(first, m2, 0.0), jnp.where(first, 0.0, m2)], axis=1)


def _unit_lower_inverse_pairs(l2):
    i = lax.broadcasted_iota(jnp.int32, (CHUNK, 2 * CHUNK), 0)
    j = lax.broadcasted_iota(jnp.int32, (CHUNK, 2 * CHUNK), 1) & (CHUNK - 1)

    def sub_blocks(s):
        return (((i >> s) & 1) == 1) & ((j >> s) == (i >> s) - 1)

    inv = jnp.where(i == j, 1.0, 0.0) - jnp.where(sub_blocks(0), l2, 0.0)
    s = 1
    while (1 << s) < CHUNK:
        x = _bmm(inv, _pair_blockdiag(jnp.where(sub_blocks(s), l2, 0.0)))
        inv = inv - _bmm(x, _pair_blockdiag(inv))
        s += 1
    return inv


def _gdn_kernel(q_ref, k_ref, v_ref, z_ref, wq_ref, wk_ref, wv_ref, g_ref, gt_ref, og_ref,
                o_ref, qbuf, kbuf, vbuf, state_ref, *, tt, hg):
    hgi = pl.program_id(1)
    t = pl.program_id(2)
    nc = tt // CHUNK
    nh = 2 * hg

    @pl.when(t == 0)
    def _():
        state_ref[...] = jnp.zeros_like(state_ref)
        qbuf[pl.ds(tt, 8), :] = jnp.zeros((8, hg * HEAD), F32)
        kbuf[pl.ds(tt, 8), :] = jnp.zeros((8, hg * HEAD), F32)
        vbuf[pl.ds(tt, 8), :] = jnp.zeros((8, nh * HEAD), F32)

    def conv_silu(x_ref, buf, w_ref):
        buf[pl.ds(0, 8), :] = buf[pl.ds(tt, 8), :]
        buf[pl.ds(8, tt), :] = x_ref[0]
        acc = w_ref[GDN_CONV_K - 1:GDN_CONV_K, :] * buf[pl.ds(8, tt), :]
        for j in range(1, GDN_CONV_K):
            tap = GDN_CONV_K - 1 - j
            acc = acc + w_ref[tap:tap + 1, :] * buf[pl.ds(8 - j, tt), :]
        return _silu(acc)

    def l2n(v):
        return v * lax.rsqrt(jnp.sum(v * v, axis=-1, keepdims=True) + NORM_EPS)

    def chunked(v):
        return v.reshape(nc, CHUNK, HEAD)

    qa = conv_silu(q_ref, qbuf, wq_ref)
    ka = conv_silu(k_ref, kbuf, wk_ref)
    va = conv_silu(v_ref, vbuf, wv_ref)

    gates = g_ref[0]
    lane = lax.broadcasted_iota(jnp.int32, gates.shape, 1)
    first_tt = lane < CHUNK
    row_i = lax.broadcasted_iota(jnp.int32, (CHUNK, 2 * CHUNK), 0)
    col_j = lax.broadcasted_iota(jnp.int32, (CHUNK, 2 * CHUNK), 1) & (CHUNK - 1)
    causal = row_i >= col_j
    strict = row_i > col_j
    zeros_rhs = jnp.zeros((nc, CHUNK, 2 * HEAD), F32)

    lmats, rhss, qgs, kgs, pmats, elasts = [], [], [], [], [], []
    for khl in range(hg):
        hs = slice(khl * HEAD, (khl + 1) * HEAD)
        qc = chunked(l2n(qa[:, hs]) * (HEAD ** -0.5))
        kc = chunked(l2n(ka[:, hs]))
        kc2 = jnp.concatenate([kc, kc], axis=1)
        kk2 = _bmm_t(kc, kc2)
        qk2 = _bmm_t(qc, kc2)
        beta_cols, gc_cols, gc_rows = [], [], []
        for e in range(2):
            hv = hgi * nh + 2 * khl + e
            beta_cols.append(jnp.sum(jnp.where(lane == hv, gates, 0.0), axis=-1, keepdims=True))
            gc_cols.append(
                jnp.sum(jnp.where(lane == hv + GDN_NV, gates, 0.0), axis=-1, keepdims=True))
            gc_rows.append(gt_ref[pl.ds(hv + GDN_NV, 1), :])
        bi2 = jnp.where(first_tt, beta_cols[0], beta_cols[1]).reshape(nc, CHUNK, HEAD)
        gci2 = jnp.where(first_tt, gc_cols[0], gc_cols[1]).reshape(nc, CHUNK, HEAD)
        gcj2 = jnp.stack(
            [jnp.concatenate([gc_rows[0][:, c * CHUNK:(c + 1) * CHUNK],
                              gc_rows[1][:, c * CHUNK:(c + 1) * CHUNK]], axis=1)
             for c in range(nc)], axis=0)
        dec2 = jnp.exp(jnp.where(causal, gci2 - gcj2, NEG_BIG))
        lmats.append(jnp.where(strict, kk2 * bi2 * dec2, 0.0))
        pmats.append((qk2 * dec2).astype(BF16))
        rhs_pair = []
        for e in range(2):
            hl = 2 * khl + e
            bi = beta_cols[e].reshape(nc, CHUNK, 1)
            gci = gc_cols[e].reshape(nc, CHUNK, 1)
            egc = jnp.exp(gci)
            g_last = gci[:, CHUNK - 1:CHUNK, :]
            vc = chunked(va[:, hl * HEAD:(hl + 1) * HEAD])
            rhs_pair.append(jnp.concatenate([vc * bi, kc * (bi * egc)], axis=-1))
            qgs.append((qc * egc).astype(BF16))
            kgs.append((kc * jnp.exp(g_last - gci)).astype(BF16))
            elasts.append(jnp.exp(g_last))
        rhss.append(jnp.concatenate(
            [jnp.concatenate([rhs_pair[0], zeros_rhs], axis=-1),
             jnp.concatenate([zeros_rhs, rhs_pair[1]], axis=-1)], axis=1))
    inv2 = _unit_lower_inverse_pairs(jnp.concatenate(lmats, axis=0))
    sol2 = _bmm(inv2, jnp.concatenate(rhss, axis=0))

    states = [state_ref[hl] for hl in range(nh)]
    o_chunks = [[] for _ in range(nh)]
    zeros_vn = jnp.zeros((CHUNK, HEAD), BF16)
    for c in range(nc):
        for khl in range(hg):
            p = khl * nc + c
            qss, vnbs = [], []
            for e in range(2):
                hl = 2 * khl + e
                u = sol2[p, :, 2 * e * HEAD:(2 * e + 1) * HEAD]
                w = sol2[p, :, (2 * e + 1) * HEAD:(2 * e + 2) * HEAD]
                ws_qs = jnp.dot(jnp.concatenate([w.astype(BF16), qgs[hl][c]], axis=0),
                                states[hl].astype(BF16), preferred_element_type=F32)
                vnb = (u - ws_qs[:CHUNK]).astype(BF16)
                states[hl] = states[hl] * elasts[hl][c] + _tdot(kgs[hl][c], vnb)
                qss.append(ws_qs[CHUNK:])
                vnbs.append(vnb)
            vn_bd = jnp.concatenate(
                [jnp.concatenate([vnbs[0], zeros_vn], axis=-1),
                 jnp.concatenate([zeros_vn, vnbs[1]], axis=-1)], axis=0)
            pv = jnp.dot(pmats[khl][c], vn_bd, preferred_element_type=F32)
            for e in range(2):
                o_chunks[2 * khl + e].append(qss[e] + pv[:, e * HEAD:(e + 1) * HEAD])
    for hl in range(nh):
        state_ref[hl] = states[hl]
        o = jnp.concatenate(o_chunks[hl], axis=0)
        z = z_ref[0, :, hl * HEAD:(hl + 1) * HEAD]
        y = o * lax.rsqrt(jnp.mean(o * o, axis=-1, keepdims=True) + NORM_EPS) * og_ref[...]
        o_ref[0, :, hl * HEAD:(hl + 1) * HEAD] = (y * _silu(z)).astype(BF16)


def gdn_core(proj, conv_w, gates, gates_t, onorm_g, bsz, seq, *, tt=512, hg=4):
    tt = min(tt, seq)
    nt = seq // tt
    nh = 2 * hg
    proj3 = proj.reshape(bsz, seq, proj.shape[1])
    qw = hg * HEAD
    vw = nh * HEAD
    kq = GDN_QK // qw
    kv = 2 * GDN_QK // vw
    kz = GDN_CONV // vw
    out = pl.pallas_call(
        functools.partial(_gdn_kernel, tt=tt, hg=hg),
        grid=(bsz, GDN_NK // hg, nt),
        in_specs=[
            pl.BlockSpec((1, tt, qw), lambda b, h, t: (b, t, h)),
            pl.BlockSpec((1, tt, qw), lambda b, h, t: (b, t, kq + h)),
            pl.BlockSpec((1, tt, vw), lambda b, h, t: (b, t, kv + h)),
            pl.BlockSpec((1, tt, vw), lambda b, h, t: (b, t, kz + h)),
            pl.BlockSpec((GDN_CONV_K, qw), lambda b, h, t: (0, h)),
            pl.BlockSpec((GDN_CONV_K, qw), lambda b, h, t: (0, kq + h)),
            pl.BlockSpec((GDN_CONV_K, vw), lambda b, h, t: (0, kv + h)),
            pl.BlockSpec((1, tt, HEAD), lambda b, h, t: (b, t, 0)),
            pl.BlockSpec((HEAD, tt), lambda b, h, t: (0, b * nt + t)),
            pl.BlockSpec((1, HEAD), lambda b, h, t: (0, 0)),
        ],
        out_specs=pl.BlockSpec((1, tt, vw), lambda b, h, t: (b, t, h)),
        out_shape=jax.ShapeDtypeStruct((bsz, seq, GDN_V), BF16),
        scratch_shapes=[pltpu.VMEM((tt + 8, qw), F32), pltpu.VMEM((tt + 8, qw), F32),
                        pltpu.VMEM((tt + 8, vw), F32), pltpu.VMEM((nh, HEAD, HEAD), F32)],
        compiler_params=_cparams("parallel", "parallel", "arbitrary"),
        name="gdn_core",
    )(proj3, proj3, proj3, proj3, conv_w, conv_w, conv_w,
      gates.reshape(bsz, seq, HEAD), gates_t, onorm_g.reshape(1, HEAD))
    return out.reshape(bsz * seq, GDN_V)


def _attn_kernel(q_ref, kp_ref, kc_ref, vp_ref, vc_ref, o_ref, lse_ref):
    lb = pl.program_id(2)
    rows = ATT_G * SPAN
    qi = lax.broadcasted_iota(jnp.int32, (rows, 2 * SPAN), 0) & (SPAN - 1)
    kj = lax.broadcasted_iota(jnp.int32, (rows, 2 * SPAN), 1)
    rel = qi + SPAN - kj
    mask = (rel >= 0) & (rel <= SPAN) & (kj >= jnp.where(lb > 0, 0, SPAN))
    scale = HEAD ** -0.5
    lse_ref[0] = jnp.zeros(lse_ref.shape[1:], F32)
    for h in range(ATT_HKV):
        hs = slice(h * HEAD, (h + 1) * HEAD)
        k2 = jnp.concatenate([kp_ref[0, :, hs], kc_ref[0, :, hs]], axis=0)
        v2 = jnp.concatenate([vp_ref[0, :, hs], vc_ref[0, :, hs]], axis=0)
        q4 = jnp.concatenate(
            [q_ref[0, :, (h * ATT_G + g) * HEAD:(h * ATT_G + g + 1) * HEAD] for g in range(ATT_G)],
            axis=0)
        s = jnp.where(mask, _dot_t(q4, k2) * scale, NEG_BIG)
        mx = jnp.max(s, axis=-1, keepdims=True)
        p = jnp.exp(s - mx)
        den = jnp.sum(p, axis=-1, keepdims=True)
        o4 = jnp.dot(p.astype(BF16), v2, preferred_element_type=F32) / den
        lse = mx + jnp.log(den)
        for g in range(ATT_G):
            hq = h * ATT_G + g
            o_ref[0, :, hq * HEAD:(hq + 1) * HEAD] = o4[g * SPAN:(g + 1) * SPAN].astype(o_ref.dtype)
            lse_ref[0, :, hq:hq + 1] = lse[g * SPAN:(g + 1) * SPAN]


def dilated_attention_branch(q, k, v, gi):
    dil = DILATIONS[gi]
    bsz, sub, _ = q.shape
    nb = sub // SPAN
    qd = ATT_HQ * HEAD
    kd = ATT_HKV * HEAD

    def cur(b, r, lb):
        return (b, lb, r)

    def prev(b, r, lb):
        return (b, jnp.maximum(lb - 1, 0), r)

    return pl.pallas_call(
        _attn_kernel,
        grid=(bsz, dil, nb),
        in_specs=[
            pl.BlockSpec((1, SPAN, qd), cur),
            pl.BlockSpec((1, SPAN, kd), prev),
            pl.BlockSpec((1, SPAN, kd), cur),
            pl.BlockSpec((1, SPAN, kd), prev),
            pl.BlockSpec((1, SPAN, kd), cur),
        ],
        out_specs=[pl.BlockSpec((1, SPAN, qd), cur), pl.BlockSpec((1, SPAN, HEAD), cur)],
        out_shape=[jax.ShapeDtypeStruct((bsz, sub, dil * qd), BF16),
                   jax.ShapeDtypeStruct((bsz, sub, dil * HEAD), F32)],
        compiler_params=_cparams("parallel", "parallel", "parallel"),
        name=f"dilated_attention_{gi}",
    )(q, k, k, v, v)


def kernel(x, c, positions, ada_w, ada_b, norm_g, mlp_w1, mlp_w2, gdn_w_in, gdn_conv_w, gdn_a_log,
           gdn_dt_bias, gdn_onorm_g, gdn_w_out, kv_norm_g, kv_ada_w, kv_ada_b, w_kv, k_norm_g,
           attn_w_q, q_norm_g, attn_w_o):
    bsz, seq, d = x.shape
    depth = ada_w.shape[0]
    n_a = gdn_w_in.shape[0]
    m = bsz * seq
    xf = x.reshape(m, d)

    mod = ada_modulation(c, ada_w, ada_b)
    kv_mod = ada_modulation(c, kv_ada_w[None], kv_ada_b[None])[0]
    cos_tab, sin_tab = rope_tables(positions)

    kv_cols = w_kv.reshape(d, N_GROUPS, 2, ATT_HKV * HEAD)
    w_kv_p = jnp.concatenate([_permute_head_lanes(kv_cols[:, :, 0]), kv_cols[:, :, 1]],
                             axis=-1).reshape(d, -1).astype(BF16)
    k_gain_p = _permute_head_lanes(k_norm_g)

    kv = None
    for layer in range(depth):
        sh1, sc1, gt1, sh2, sc2, gt2 = [mod[layer, :, i * d:(i + 1) * d] for i in range(6)]
        if layer < n_a:
            w_in = jnp.pad(gdn_w_in[layer], ((0, 0), (0, GDN_PROJ_PAD - gdn_w_in.shape[2])))
            proj = norm_mod_matmul(xf, norm_g[layer, 0], sh1, sc1, w_in.astype(BF16), seq,
                                   out_dtype=F32, tm=512, tn=GDN_PROJ_PAD // 5)
            gates, gates_t = gdn_gates(proj, gdn_a_log[layer], gdn_dt_bias[layer])
            o = gdn_core(proj, gdn_conv_w[layer], gates, gates_t, gdn_onorm_g[layer], bsz, seq)
            xf = matmul_gated_residual(o, gdn_w_out[layer].astype(BF16), xf, gt1, seq)
        else:
            if kv is None:
                kv = branch_projection(xf, kv_norm_g, kv_mod[:, :d], kv_mod[:, d:], w_kv_p,
                                       k_gain_p, cos_tab, sin_tab,
                                       [(ATT_HKV, True), (ATT_HKV, False)], bsz, seq)
            j = layer - n_a
            qs = branch_projection(xf, norm_g[layer, 0], sh1, sc1,
                                   _permute_head_lanes(attn_w_q[j]).astype(BF16),
                                   _permute_head_lanes(q_norm_g[j]), cos_tab, sin_tab,
                                   [(ATT_HQ, True)], bsz, seq)
            branches = [dilated_attention_branch(qs[gi], kv[2 * gi], kv[2 * gi + 1], gi)
                        for gi in range(N_GROUPS)]
            xf = merge_matmul_gated_residual([b[0] for b in branches], [b[1] for b in branches],
                                             attn_w_o[j].astype(BF16), xf, gt1, seq)
        xf = mlp_sublayer(xf, norm_g[layer, 1], sh2, sc2, gt2, mlp_w1[layer].astype(BF16),
                          mlp_w2[layer].astype(BF16), seq)
    return xf.reshape(bsz, seq, d)
```

```python
import functools

import numpy as np
import jax
import jax.numpy as jnp
from jax import lax
from jax.experimental import pallas as pl
from jax.experimental.pallas import tpu as pltpu

F32 = jnp.float32
BF16 = jnp.bfloat16

NORM_EPS = 1e-6
HEAD = 128
GDN_NK = 16
GDN_NV = 32
GDN_CONV_K = 4
CHUNK = 64
GDN_QK = GDN_NK * HEAD
GDN_V = GDN_NV * HEAD
GDN_CONV = 2 * GDN_QK + GDN_V
GDN_BA_COL = GDN_CONV + GDN_V
GDN_PROJ_PAD = GDN_BA_COL + 512
ATT_HQ = 16
ATT_HKV = 4
ATT_G = ATT_HQ // ATT_HKV
DILATIONS = (1, 4, 16)
SPAN = 128
N_GROUPS = 3
ROPE_THETA = 500000.0
ROT_DIM = HEAD // 4
NEG_BIG = -1e30

VMEM_LIMIT = 56 * 1024 * 1024


def _cparams(*sem):
    return pltpu.CompilerParams(dimension_semantics=sem, vmem_limit_bytes=VMEM_LIMIT)


def _silu(v):
    return v * jax.nn.sigmoid(v)


def _ada_kernel(c_ref, w_ref, b_ref, o_ref):
    c = c_ref[...]
    ca = _silu(c).astype(BF16)
    w = w_ref[0].astype(BF16)
    o_ref[0] = jnp.dot(ca, w, preferred_element_type=F32) + b_ref[0]


def ada_modulation(c, w, b, tn=1024):
    nl, d, n = w.shape
    bsz = c.shape[0]
    return pl.pallas_call(
        _ada_kernel,
        grid=(nl, n // tn),
        in_specs=[
            pl.BlockSpec((bsz, d), lambda l, j: (0, 0)),
            pl.BlockSpec((1, d, tn), lambda l, j: (l, 0, j)),
            pl.BlockSpec((1, 1, tn), lambda l, j: (l, 0, j)),
        ],
        out_specs=pl.BlockSpec((1, bsz, tn), lambda l, j: (l, 0, j)),
        out_shape=jax.ShapeDtypeStruct((nl, bsz, n), F32),
        compiler_params=_cparams("parallel", "parallel"),
        name="ada_modulation",
    )(c, w, b.reshape(nl, 1, n))


ROT_HALF = ROT_DIM // 2
HEAD_LANE_ORDER = np.concatenate([np.arange(0, ROT_HALF), np.arange(ROT_DIM, HEAD // 2 + ROT_HALF),
                                  np.arange(ROT_HALF, ROT_DIM),
                                  np.arange(HEAD // 2 + ROT_HALF, HEAD)])


def _permute_head_lanes(a):
    lead = a.shape[:-1]
    a = a.reshape(lead + (a.shape[-1] // HEAD, HEAD))
    parts = [a[..., 0:ROT_HALF], a[..., ROT_DIM:HEAD // 2 + ROT_HALF], a[..., ROT_HALF:ROT_DIM],
             a[..., HEAD // 2 + ROT_HALF:]]
    return jnp.concatenate(parts, axis=-1).reshape(lead + (-1,))


def _rope_kernel(pos_ref, inv_ref, cos_ref, sin_ref):
    ang = pos_ref[...].astype(F32) * inv_ref[...]
    lane = lax.broadcasted_iota(jnp.int32, ang.shape, 1)
    c = jnp.cos(ang)
    s = jnp.sin(ang)
    first = lane < ROT_HALF
    second = (lane >= HEAD // 2) & (lane < HEAD // 2 + ROT_HALF)
    cos_ref[...] = jnp.where(first | second, c, 1.0)
    sin_ref[...] = jnp.where(first, -s, jnp.where(second, s, 0.0))


def rope_tables(positions, ts=2048):
    m = positions.size
    inv = ROPE_THETA ** (-np.arange(0, ROT_DIM, 2, dtype=np.float32) / ROT_DIM)
    inv_row = np.zeros((1, HEAD), np.float32)
    inv_row[0, :ROT_HALF] = inv
    inv_row[0, HEAD // 2:HEAD // 2 + ROT_HALF] = inv
    ts = min(ts, m)
    return pl.pallas_call(
        _rope_kernel,
        grid=(m // ts,),
        in_specs=[pl.BlockSpec((ts, 1), lambda i: (i, 0)),
                  pl.BlockSpec((1, HEAD), lambda i: (0, 0))],
        out_specs=[pl.BlockSpec((ts, HEAD), lambda i: (i, 0))] * 2,
        out_shape=[jax.ShapeDtypeStruct((m, HEAD), F32)] * 2,
        compiler_params=_cparams("parallel"),
        name="rope_tables",
    )(positions.reshape(m, 1), jnp.asarray(inv_row))


def _head_norm_rope(t, gain, cos_t, sin_t):
    y = t * lax.rsqrt(jnp.mean(t * t, axis=-1, keepdims=True) + NORM_EPS) * gain
    return y * cos_t + pltpu.roll(y, HEAD // 2, 1) * sin_t


def _norm_modulate(x, gain, shift, scale):
    y = x * lax.rsqrt(jnp.mean(x * x, axis=-1, keepdims=True) + NORM_EPS) * gain
    return y * (1.0 + scale) + shift


def _nmm_kernel(x_ref, sh_ref, sc_ref, g_ref, w_ref, o_ref, h_ref):
    @pl.when(pl.program_id(1) == 0)
    def _():
        h_ref[...] = _norm_modulate(x_ref[...], g_ref[...], sh_ref[0], sc_ref[0]).astype(BF16)

    o_ref[...] = jnp.dot(h_ref[...], w_ref[...], preferred_element_type=F32).astype(o_ref.dtype)


def norm_mod_matmul(x, gain, shift, scale, w, seq, *, out_dtype, tm=1024, tn=512):
    m, d = x.shape
    n = w.shape[1]
    bsz = shift.shape[0]
    tpb = seq // tm
    return pl.pallas_call(
        _nmm_kernel,
        grid=(m // tm, n // tn),
        in_specs=[
            pl.BlockSpec((tm, d), lambda i, j: (i, 0)),
            pl.BlockSpec((1, 1, d), lambda i, j: (i // tpb, 0, 0)),
            pl.BlockSpec((1, 1, d), lambda i, j: (i // tpb, 0, 0)),
            pl.BlockSpec((1, d), lambda i, j: (0, 0)),
            pl.BlockSpec((d, tn), lambda i, j: (0, j)),
        ],
        out_specs=pl.BlockSpec((tm, tn), lambda i, j: (i, j)),
        out_shape=jax.ShapeDtypeStruct((m, n), out_dtype),
        scratch_shapes=[pltpu.VMEM((tm, d), BF16)],
        compiler_params=_cparams("parallel", "arbitrary"),
        name="norm_mod_matmul",
    )(x, shift.reshape(bsz, 1, d), scale.reshape(bsz, 1, d), gain.reshape(1, d), w)


def _dilated_rows(r, n, dil):
    return pl.ds(r, n, stride=dil) if dil > 1 else pl.ds(0, n)


def _residue_major_perm(tm, dil):
    n = tm // dil
    j = np.arange(tm)
    p = np.zeros((tm, tm), np.float32)
    p[j, (j % n) * dil + j // n] = 1.0
    return p


def _branch_proj_kernel(x_ref, sh_ref, sc_ref, g_ref, w_ref, cos_ref, sin_ref, hg_ref, p1_ref,
                        p2_ref, *rest, segments):
    n_out = len(segments) * N_GROUPS
    outs = rest[:n_out]
    h_ref, y_ref = rest[n_out:]
    perms = (None, p1_ref, p2_ref)
    grp = pl.program_id(1)
    tm = x_ref.shape[0]

    @pl.when(grp == 0)
    def _():
        h_ref[...] = _norm_modulate(x_ref[...], g_ref[...], sh_ref[0], sc_ref[0]).astype(BF16)

    acc = jnp.dot(h_ref[...], w_ref[...], preferred_element_type=F32)
    gain = hg_ref[0]
    cos_t = cos_ref[...]
    sin_t = sin_ref[...]
    ri = lax.broadcasted_iota(jnp.int32, (2 * HEAD, 2 * HEAD), 0) // HEAD
    ci = lax.broadcasted_iota(jnp.int32, (2 * HEAD, 2 * HEAD), 1) // HEAD
    ones_bd = jnp.where(ri == ci, 1.0, 0.0).astype(BF16)
    gain2 = jnp.concatenate([gain, gain], axis=-1)
    cos2 = jnp.concatenate([cos_t, cos_t], axis=-1)
    sin2 = jnp.concatenate([sin_t, sin_t], axis=-1)
    hh = 0
    for heads, roped in segments:
        for _ in range(heads // 2):
            cols = slice(hh * HEAD, (hh + 2) * HEAD)
            t = acc[:, cols]
            if roped:
                ss = jnp.dot((t * t).astype(BF16), ones_bd, preferred_element_type=F32)
                y = t * lax.rsqrt(ss * (1.0 / HEAD) + NORM_EPS) * gain2
                swapped = jnp.concatenate([pltpu.roll(y[:, :HEAD], HEAD // 2, 1),
                                           pltpu.roll(y[:, HEAD:], HEAD // 2, 1)], axis=-1)
                t = y * cos2 + swapped * sin2
            y_ref[:, cols] = t.astype(BF16)
            hh += 2

    def write_branch(gi):
        dil = DILATIONS[gi]
        n = tm // dil
        y = y_ref[...]
        if dil > 1:
            y = jnp.dot(perms[gi][...], y, preferred_element_type=F32).astype(BF16)
        c0 = 0
        for si, (heads, _) in enumerate(segments):
            o_ref = outs[gi * len(segments) + si]
            width = heads * HEAD
            for r in range(dil):
                o_ref[0, :, r * width:(r + 1) * width] = y[r * n:(r + 1) * n, c0:c0 + width]
            c0 += width

    for gi in range(N_GROUPS):
        pl.when(grp == gi)(functools.partial(write_branch, gi))


def branch_projection(x, gain, shift, scale, w, head_gain, cos_tab, sin_tab, segments, bsz, seq,
                      *, tm=512):
    m, d = x.shape
    cgrp = w.shape[1] // N_GROUPS
    nheads = cgrp // HEAD
    tpb = seq // tm
    out_specs, out_shapes = [], []
    for gi in range(N_GROUPS):
        dil = DILATIONS[gi]
        for heads, _ in segments:
            width = dil * heads * HEAD
            out_specs.append(pl.BlockSpec((1, tm // dil, width),
                                          lambda i, g: (i // tpb, i % tpb, 0)))
            out_shapes.append(jax.ShapeDtypeStruct((bsz, seq // dil, width), BF16))
    return pl.pallas_call(
        functools.partial(_branch_proj_kernel, segments=tuple(segments)),
        grid=(m // tm, N_GROUPS),
        in_specs=[
            pl.BlockSpec((tm, d), lambda i, g: (i, 0)),
            pl.BlockSpec((1, 1, d), lambda i, g: (i // tpb, 0, 0)),
            pl.BlockSpec((1, 1, d), lambda i, g: (i // tpb, 0, 0)),
            pl.BlockSpec((1, d), lambda i, g: (0, 0)),
            pl.BlockSpec((d, cgrp), lambda i, g: (0, g)),
            pl.BlockSpec((tm, HEAD), lambda i, g: (i, 0)),
            pl.BlockSpec((tm, HEAD), lambda i, g: (i, 0)),
            pl.BlockSpec((1, 1, HEAD), lambda i, g: (g, 0, 0)),
            pl.BlockSpec((tm, tm), lambda i, g: (0, 0)),
            pl.BlockSpec((tm, tm), lambda i, g: (0, 0)),
        ],
        out_specs=out_specs,
        out_shape=out_shapes,
        scratch_shapes=[pltpu.VMEM((tm, d), BF16), pltpu.VMEM((tm, cgrp), BF16)],
        compiler_params=_cparams("parallel", "arbitrary"),
        name="branch_projection",
    )(x, shift.reshape(bsz, 1, d), scale.reshape(bsz, 1, d), gain.reshape(1, d), w,
      cos_tab, sin_tab, head_gain.reshape(N_GROUPS, 1, HEAD),
      jnp.asarray(_residue_major_perm(tm, DILATIONS[1]), BF16),
      jnp.asarray(_residue_major_perm(tm, DILATIONS[2]), BF16))


def _mm_res_kernel(a_ref, w_ref, x_ref, gt_ref, o_ref):
    y = jnp.dot(a_ref[...], w_ref[...], preferred_element_type=F32)
    o_ref[...] = x_ref[...] + gt_ref[0] * y


def matmul_gated_residual(a, w, x, gate, seq, *, tm=1024, tn=512):
    m, k = a.shape
    d = w.shape[1]
    bsz = gate.shape[0]
    tpb = seq // tm
    return pl.pallas_call(
        _mm_res_kernel,
        grid=(m // tm, d // tn),
        in_specs=[
            pl.BlockSpec((tm, k), lambda i, j: (i, 0)),
            pl.BlockSpec((k, tn), lambda i, j: (0, j)),
            pl.BlockSpec((tm, tn), lambda i, j: (i, j)),
            pl.BlockSpec((1, 1, tn), lambda i, j: (i // tpb, 0, j)),
        ],
        out_specs=pl.BlockSpec((tm, tn), lambda i, j: (i, j)),
        out_shape=jax.ShapeDtypeStruct((m, d), F32),
        compiler_params=_cparams("parallel", "parallel"),
        name="matmul_gated_residual",
    )(a, w, x, gate.reshape(bsz, 1, d))


def _merge_mm_res_kernel(o0_ref, o1_ref, o2_ref, l0_ref, l1_ref, l2_ref, w_ref, x_ref, gt_ref,
                         p1_ref, p2_ref, o_ref, a_ref, stack_ref, onat_ref, l3_ref):
    j = pl.program_id(1)
    tm = x_ref.shape[0]
    o_refs = (o0_ref, o1_ref, o2_ref)
    l_refs = (l0_ref, l1_ref, l2_ref)
    perms = (None, p1_ref, p2_ref)

    @pl.when(j == 0)
    def _():
        for gi, dil in enumerate(DILATIONS):
            n = tm // dil
            for r in range(dil):
                l3_ref[gi, _dilated_rows(r, n, dil), :] = l_refs[gi][0, :, r * HEAD:(r + 1) * HEAD]
        l0, l1, l2 = l3_ref[0], l3_ref[1], l3_ref[2]
        mx = jnp.maximum(jnp.maximum(l0, l1), l2)
        e0 = jnp.exp(l0 - mx)
        e1 = jnp.exp(l1 - mx)
        e2 = jnp.exp(l2 - mx)
        den = e0 + e1 + e2
        l3_ref[0] = e0 / den
        l3_ref[1] = e1 / den
        l3_ref[2] = e2 / den
        kdim = a_ref.shape[1]
        for gi in range(1, N_GROUPS):
            dil = DILATIONS[gi]
            n = tm // dil
            for r in range(dil):
                stack_ref[r * n:(r + 1) * n, :] = o_refs[gi][0, :, r * kdim:(r + 1) * kdim]
            onat_ref[gi - 1] = jnp.dot(perms[gi][...], stack_ref[...],
                                       preferred_element_type=F32).astype(BF16)
        for hq in range(ATT_HQ):
            hs = slice(hq * HEAD, (hq + 1) * HEAD)
            merged = l3_ref[0][:, hq:hq + 1] * o0_ref[0, :, hs].astype(F32)
            for gi in range(1, N_GROUPS):
                merged = merged + l3_ref[gi][:, hq:hq + 1] * onat_ref[gi - 1, :, hs].astype(F32)
            a_ref[:, hs] = merged.astype(BF16)

    y = jnp.dot(a_ref[...], w_ref[...], preferred_element_type=F32)
    o_ref[...] = x_ref[...] + gt_ref[0] * y


def merge_matmul_gated_residual(outs, lses, w, x, gate, seq, *, tm=512, tn=512):
    m, d = x.shape
    k = w.shape[0]
    bsz = gate.shape[0]
    tpb = seq // tm
    o_specs = [pl.BlockSpec((1, tm // dil, dil * k), lambda i, j: (i // tpb, i % tpb, 0))
               for dil in DILATIONS]
    l_specs = [pl.BlockSpec((1, tm // dil, dil * HEAD), lambda i, j: (i // tpb, i % tpb, 0))
               for dil in DILATIONS]
    return pl.pallas_call(
        _merge_mm_res_kernel,
        grid=(m // tm, d // tn),
        in_specs=o_specs + l_specs + [
            pl.BlockSpec((k, tn), lambda i, j: (0, j)),
            pl.BlockSpec((tm, tn), lambda i, j: (i, j)),
            pl.BlockSpec((1, 1, tn), lambda i, j: (i // tpb, 0, j)),
            pl.BlockSpec((tm, tm), lambda i, j: (0, 0)),
            pl.BlockSpec((tm, tm), lambda i, j: (0, 0))],
        out_specs=pl.BlockSpec((tm, tn), lambda i, j: (i, j)),
        out_shape=jax.ShapeDtypeStruct((m, d), F32),
        scratch_shapes=[pltpu.VMEM((tm, k), BF16), pltpu.VMEM((tm, k), BF16),
                        pltpu.VMEM((N_GROUPS - 1, tm, k), BF16),
                        pltpu.VMEM((N_GROUPS, tm, HEAD), F32)],
        compiler_params=_cparams("parallel", "arbitrary"),
        name="merge_matmul_gated_residual",
    )(*outs, *lses, w, x, gate.reshape(bsz, 1, d),
      jnp.asarray(_residue_major_perm(tm, DILATIONS[1]).T, BF16),
      jnp.asarray(_residue_major_perm(tm, DILATIONS[2]).T, BF16))


def _mlp_kernel(x_ref, sh_ref, sc_ref, gt_ref, g_ref, w1_ref, w2_ref, o_ref, h_ref, acc_ref,
                a_ref, *, nf):
    f = pl.program_id(1)

    def up():
        a = jnp.dot(h_ref[...], w1_ref[...], preferred_element_type=F32)
        return jnp.square(jnp.maximum(a, 0.0)).astype(BF16)

    def down(slot):
        acc_ref[...] += jnp.dot(a_ref[slot], w2_ref[...], preferred_element_type=F32)

    @pl.when(f == 0)
    def _():
        h_ref[...] = _norm_modulate(x_ref[...], g_ref[...], sh_ref[0], sc_ref[0]).astype(BF16)
        acc_ref[...] = jnp.zeros_like(acc_ref)
        a_ref[0] = up()

    for slot in range(2):
        @pl.when((f > 0) & (f < nf) & (f % 2 == slot))
        def _(slot=slot):
            a_new = up()
            down(1 - slot)
            a_ref[slot] = a_new

    @pl.when(f == nf)
    def _():
        down((nf - 1) % 2)
        o_ref[...] = x_ref[...] + gt_ref[0] * acc_ref[...]


def mlp_sublayer(x, gain, shift, scale, gate, w1, w2, seq, *, tm=512, tf=1024):
    m, d = x.shape
    dff = w1.shape[1]
    nf = dff // tf
    bsz = gate.shape[0]
    tpb = seq // tm
    vec = pl.BlockSpec((1, 1, d), lambda i, f: (i // tpb, 0, 0))
    return pl.pallas_call(
        functools.partial(_mlp_kernel, nf=nf),
        grid=(m // tm, nf + 1),
        in_specs=[
            pl.BlockSpec((tm, d), lambda i, f: (i, 0)),
            vec, vec, vec,
            pl.BlockSpec((1, d), lambda i, f: (0, 0)),
            pl.BlockSpec((d, tf), lambda i, f: (0, jnp.minimum(f, nf - 1))),
            pl.BlockSpec((tf, d), lambda i, f: (jnp.maximum(f - 1, 0), 0)),
        ],
        out_specs=pl.BlockSpec((tm, d), lambda i, f: (i, 0)),
        out_shape=jax.ShapeDtypeStruct((m, d), F32),
        scratch_shapes=[pltpu.VMEM((tm, d), BF16), pltpu.VMEM((tm, d), F32),
                        pltpu.VMEM((2, tm, tf), BF16)],
        compiler_params=_cparams("parallel", "arbitrary"),
        name="mlp_sublayer",
    )(x, shift.reshape(bsz, 1, d), scale.reshape(bsz, 1, d), gate.reshape(bsz, 1, d),
      gain.reshape(1, d), w1, w2)


def _softplus(v):
    return jnp.maximum(v, 0.0) + jnp.log1p(jnp.exp(-jnp.abs(v)))


def _gates_kernel(ba_ref, alog_ref, dtb_ref, tril_ref, g_ref, gt_ref):
    ba = ba_ref[...]
    lane = lax.broadcasted_iota(jnp.int32, ba.shape, 1)
    is_alpha = (lane >= GDN_NV) & (lane < 2 * GDN_NV)
    g = jnp.where(is_alpha, -jnp.exp(alog_ref[...]) * _softplus(ba + dtb_ref[...]), 0.0)
    gc = jnp.dot(tril_ref[...], g, preferred_element_type=F32, precision=lax.Precision.HIGHEST)
    out = jnp.where(lane < GDN_NV, jax.nn.sigmoid(ba), gc)
    g_ref[...] = out
    gt_ref[...] = out.T


def gdn_gates(proj, a_log, dt_bias, tg=512):
    m = proj.shape[0]
    tg = min(tg, m)
    alog_row = jnp.zeros((1, HEAD), F32).at[0, GDN_NV:2 * GDN_NV].set(a_log)
    dtb_row = jnp.zeros((1, HEAD), F32).at[0, GDN_NV:2 * GDN_NV].set(dt_bias)
    idx = np.arange(tg)
    tril = ((idx[:, None] >= idx[None, :]) & (idx[:, None] // CHUNK == idx[None, :] // CHUNK))
    return pl.pallas_call(
        _gates_kernel,
        grid=(m // tg,),
        in_specs=[pl.BlockSpec((tg, HEAD), lambda i: (i, GDN_BA_COL // HEAD)),
                  pl.BlockSpec((1, HEAD), lambda i: (0, 0)),
                  pl.BlockSpec((1, HEAD), lambda i: (0, 0)),
                  pl.BlockSpec((tg, tg), lambda i: (0, 0))],
        out_specs=[pl.BlockSpec((tg, HEAD), lambda i: (i, 0)),
                   pl.BlockSpec((HEAD, tg), lambda i: (0, i))],
        out_shape=[jax.ShapeDtypeStruct((m, HEAD), F32), jax.ShapeDtypeStruct((HEAD, m), F32)],
        compiler_params=_cparams("parallel"),
        name="gdn_gates",
    )(proj, alog_row, dtb_row, jnp.asarray(tril.astype(np.float32)))


def _dot_t(a, b):
    return lax.dot_general(a, b, (((1,), (1,)), ((), ())), preferred_element_type=F32)


def _tdot(a, b):
    return lax.dot_general(a, b, (((0,), (0,)), ((), ())), preferred_element_type=F32)


def _bmm(a, b):
    return jnp.einsum("bij,bjk->bik", a.astype(BF16), b.astype(BF16),
                      preferred_element_type=F32)


def _bmm_t(a, b):
    return jnp.einsum("bik,bjk->bij", a.astype(BF16), b.astype(BF16),
                      preferred_element_type=F32)


def _pair_blockdiag(m2):
    first = lax.broadcasted_iota(jnp.int32, m2.shape[1:], 1) < CHUNK
    return jnp.concatenate([jnp.where(first, m2, 0.0), jnp.where(first, 0.0, m2)], axis=1)


def _unit_lower_inverse_pairs(l2):
    i = lax.broadcasted_iota(jnp.int32, (CHUNK, 2 * CHUNK), 0)
    j = lax.broadcasted_iota(jnp.int32, (CHUNK, 2 * CHUNK), 1) & (CHUNK - 1)

    def sub_blocks(s):
        return (((i >> s) & 1) == 1) & ((j >> s) == (i >> s) - 1)

    inv = jnp.where(i == j, 1.0, 0.0) - jnp.where(sub_blocks(0), l2, 0.0)
    s = 1
    while (1 << s) < CHUNK:
        x = _bmm(inv, _pair_blockdiag(jnp.where(sub_blocks(s), l2, 0.0)))
        inv = inv - _bmm(x, _pair_blockdiag(inv))
        s += 1
    return inv


def _gdn_kernel(q_ref, k_ref, v_ref, z_ref, wq_ref, wk_ref, wv_ref, g_ref, gt_ref, og_ref,
                o_ref, qbuf, kbuf, vbuf, state_ref, *, tt, hg):
    hgi = pl.program_id(1)
    t = pl.program_id(2)
    nc = tt // CHUNK
    nh = 2 * hg

    @pl.when(t == 0)
    def _():
        state_ref[...] = jnp.zeros_like(state_ref)
        qbuf[pl.ds(tt, 8), :] = jnp.zeros((8, hg * HEAD), F32)
        kbuf[pl.ds(tt, 8), :] = jnp.zeros((8, hg * HEAD), F32)
        vbuf[pl.ds(tt, 8), :] = jnp.zeros((8, nh * HEAD), F32)

    def conv_silu(x_ref, buf, w_ref):
        buf[pl.ds(0, 8), :] = buf[pl.ds(tt, 8), :]
        buf[pl.ds(8, tt), :] = x_ref[0]
        acc = w_ref[GDN_CONV_K - 1:GDN_CONV_K, :] * buf[pl.ds(8, tt), :]
        for j in range(1, GDN_CONV_K):
            tap = GDN_CONV_K - 1 - j
            acc = acc + w_ref[tap:tap + 1, :] * buf[pl.ds(8 - j, tt), :]
        return _silu(acc)

    def l2n(v):
        return v * lax.rsqrt(jnp.sum(v * v, axis=-1, keepdims=True) + NORM_EPS)

    def chunked(v):
        return v.reshape(nc, CHUNK, HEAD)

    qa = conv_silu(q_ref, qbuf, wq_ref)
    ka = conv_silu(k_ref, kbuf, wk_ref)
    va = conv_silu(v_ref, vbuf, wv_ref)

    gates = g_ref[0]
    lane = lax.broadcasted_iota(jnp.int32, gates.shape, 1)
    first_tt = lane < CHUNK
    row_i = lax.broadcasted_iota(jnp.int32, (CHUNK, 2 * CHUNK), 0)
    col_j = lax.broadcasted_iota(jnp.int32, (CHUNK, 2 * CHUNK), 1) & (CHUNK - 1)
    causal = row_i >= col_j
    strict = row_i > col_j
    zeros_rhs = jnp.zeros((nc, CHUNK, 2 * HEAD), F32)

    lmats, rhss, qgs, kgs, pmats, elasts = [], [], [], [], [], []
    for khl in range(hg):
        hs = slice(khl * HEAD, (khl + 1) * HEAD)
        qc = chunked(l2n(qa[:, hs]) * (HEAD ** -0.5))
        kc = chunked(l2n(ka[:, hs]))
        kc2 = jnp.concatenate([kc, kc], axis=1)
        kk2 = _bmm_t(kc, kc2)
        qk2 = _bmm_t(qc, kc2)
        beta_cols, gc_cols, gc_rows = [], [], []
        for e in range(2):
            hv = hgi * nh + 2 * khl + e
            beta_cols.append(jnp.sum(jnp.where(lane == hv, gates, 0.0), axis=-1, keepdims=True))
            gc_cols.append(
                jnp.sum(jnp.where(lane == hv + GDN_NV, gates, 0.0), axis=-1, keepdims=True))
            gc_rows.append(gt_ref[pl.ds(hv + GDN_NV, 1), :])
        bi2 = jnp.where(first_tt, beta_cols[0], beta_cols[1]).reshape(nc, CHUNK, HEAD)
        gci2 = jnp.where(first_tt, gc_cols[0], gc_cols[1]).reshape(nc, CHUNK, HEAD)
        gcj2 = jnp.stack(
            [jnp.concatenate([gc_rows[0][:, c * CHUNK:(c + 1) * CHUNK],
                              gc_rows[1][:, c * CHUNK:(c + 1) * CHUNK]], axis=1)
             for c in range(nc)], axis=0)
        dec2 = jnp.exp(jnp.where(causal, gci2 - gcj2, NEG_BIG))
        lmats.append(jnp.where(strict, kk2 * bi2 * dec2, 0.0))
        pmats.append((qk2 * dec2).astype(BF16))
        rhs_pair = []
        for e in range(2):
            hl = 2 * khl + e
            bi = beta_cols[e].reshape(nc, CHUNK, 1)
            gci = gc_cols[e].reshape(nc, CHUNK, 1)
            egc = jnp.exp(gci)
            g_last = gci[:, CHUNK - 1:CHUNK, :]
            vc = chunked(va[:, hl * HEAD:(hl + 1) * HEAD])
            rhs_pair.append(jnp.concatenate([vc * bi, kc * (bi * egc)], axis=-1))
            qgs.append((qc * egc).astype(BF16))
            kgs.append((kc * jnp.exp(g_last - gci)).astype(BF16))
            elasts.append(jnp.exp(g_last))
        rhss.append(jnp.concatenate(
            [jnp.concatenate([rhs_pair[0], zeros_rhs], axis=-1),
             jnp.concatenate([zeros_rhs, rhs_pair[1]], axis=-1)], axis=1))
    inv2 = _unit_lower_inverse_pairs(jnp.concatenate(lmats, axis=0))
    sol2 = _bmm(inv2, jnp.concatenate(rhss, axis=0))

    states = [state_ref[hl] for hl in range(nh)]
    o_chunks = [[] for _ in range(nh)]
    zeros_vn = jnp.zeros((CHUNK, HEAD), BF16)
    for c in range(nc):
        for khl in range(hg):
            p = khl * nc + c
            qss, vnbs = [], []
            for e in range(2):
                hl = 2 * khl + e
                u = sol2[p, :, 2 * e * HEAD:(2 * e + 1) * HEAD]
                w = sol2[p, :, (2 * e + 1) * HEAD:(2 * e + 2) * HEAD]
                ws_qs = jnp.dot(jnp.concatenate([w.astype(BF16), qgs[hl][c]], axis=0),
                                states[hl].astype(BF16), preferred_element_type=F32)
                vnb = (u - ws_qs[:CHUNK]).astype(BF16)
                states[hl] = states[hl] * elasts[hl][c] + _tdot(kgs[hl][c], vnb)
                qss.append(ws_qs[CHUNK:])
                vnbs.append(vnb)
            vn_bd = jnp.concatenate(
                [jnp.concatenate([vnbs[0], zeros_vn], axis=-1),
                 jnp.concatenate([zeros_vn, vnbs[1]], axis=-1)], axis=0)
            pv = jnp.dot(pmats[khl][c], vn_bd, preferred_element_type=F32)
            for e in range(2):
                o_chunks[2 * khl + e].append(qss[e] + pv[:, e * HEAD:(e + 1) * HEAD])
    for hl in range(nh):
        state_ref[hl] = states[hl]
        o = jnp.concatenate(o_chunks[hl], axis=0)
        z = z_ref[0, :, hl * HEAD:(hl + 1) * HEAD]
        y = o * lax.rsqrt(jnp.mean(o * o, axis=-1, keepdims=True) + NORM_EPS) * og_ref[...]
        o_ref[0, :, hl * HEAD:(hl + 1) * HEAD] = (y * _silu(z)).astype(BF16)


def gdn_core(proj, conv_w, gates, gates_t, onorm_g, bsz, seq, *, tt=512, hg=4):
    tt = min(tt, seq)
    nt = seq // tt
    nh = 2 * hg
    proj3 = proj.reshape(bsz, seq, proj.shape[1])
    qw = hg * HEAD
    vw = nh * HEAD
    kq = GDN_QK // qw
    kv = 2 * GDN_QK // vw
    kz = GDN_CONV // vw
    out = pl.pallas_call(
        functools.partial(_gdn_kernel, tt=tt, hg=hg),
        grid=(bsz, GDN_NK // hg, nt),
        in_specs=[
            pl.BlockSpec((1, tt, qw), lambda b, h, t: (b, t, h)),
            pl.BlockSpec((1, tt, qw), lambda b, h, t: (b, t, kq + h)),
            pl.BlockSpec((1, tt, vw), lambda b, h, t: (b, t, kv + h)),
            pl.BlockSpec((1, tt, vw), lambda b, h, t: (b, t, kz + h)),
            pl.BlockSpec((GDN_CONV_K, qw), lambda b, h, t: (0, h)),
            pl.BlockSpec((GDN_CONV_K, qw), lambda b, h, t: (0, kq + h)),
            pl.BlockSpec((GDN_CONV_K, vw), lambda b, h, t: (0, kv + h)),
            pl.BlockSpec((1, tt, HEAD), lambda b, h, t: (b, t, 0)),
            pl.BlockSpec((HEAD, tt), lambda b, h, t: (0, b * nt + t)),
            pl.BlockSpec((1, HEAD), lambda b, h, t: (0, 0)),
        ],
        out_specs=pl.BlockSpec((1, tt, vw), lambda b, h, t: (b, t, h)),
        out_shape=jax.ShapeDtypeStruct((bsz, seq, GDN_V), BF16),
        scratch_shapes=[pltpu.VMEM((tt + 8, qw), F32), pltpu.VMEM((tt + 8, qw), F32),
                        pltpu.VMEM((tt + 8, vw), F32), pltpu.VMEM((nh, HEAD, HEAD), F32)],
        compiler_params=_cparams("parallel", "parallel", "arbitrary"),
        name="gdn_core",
    )(proj3, proj3, proj3, proj3, conv_w, conv_w, conv_w,
      gates.reshape(bsz, seq, HEAD), gates_t, onorm_g.reshape(1, HEAD))
    return out.reshape(bsz * seq, GDN_V)


def _attn_kernel(q_ref, kp_ref, kc_ref, vp_ref, vc_ref, o_ref, lse_ref):
    lb = pl.program_id(2)
    rows = ATT_G * SPAN
    qi = lax.broadcasted_iota(jnp.int32, (rows, 2 * SPAN), 0) & (SPAN - 1)
    kj = lax.broadcasted_iota(jnp.int32, (rows, 2 * SPAN), 1)
    rel = qi + SPAN - kj
    mask = (rel >= 0) & (rel <= SPAN) & (kj >= jnp.where(lb > 0, 0, SPAN))
    scale = HEAD ** -0.5
    lse_ref[0] = jnp.zeros(lse_ref.shape[1:], F32)
    for h in range(ATT_HKV):
        hs = slice(h * HEAD, (h + 1) * HEAD)
        k2 = jnp.concatenate([kp_ref[0, :, hs], kc_ref[0, :, hs]], axis=0)
        v2 = jnp.concatenate([vp_ref[0, :, hs], vc_ref[0, :, hs]], axis=0)
        q4 = jnp.concatenate(
            [q_ref[0, :, (h * ATT_G + g) * HEAD:(h * ATT_G + g + 1) * HEAD] for g in range(ATT_G)],
            axis=0)
        s = jnp.where(mask, _dot_t(q4, k2) * scale, NEG_BIG)
        mx = jnp.max(s, axis=-1, keepdims=True)
        p = jnp.exp(s - mx)
        den = jnp.sum(p, axis=-1, keepdims=True)
        o4 = jnp.dot(p.astype(BF16), v2, preferred_element_type=F32) / den
        lse = mx + jnp.log(den)
        for g in range(ATT_G):
            hq = h * ATT_G + g
            o_ref[0, :, hq * HEAD:(hq + 1) * HEAD] = o4[g * SPAN:(g + 1) * SPAN].astype(o_ref.dtype)
            lse_ref[0, :, hq:hq + 1] = lse[g * SPAN:(g + 1) * SPAN]


def dilated_attention_branch(q, k, v, gi):
    dil = DILATIONS[gi]
    bsz, sub, _ = q.shape
    nb = sub // SPAN
    qd = ATT_HQ * HEAD
    kd = ATT_HKV * HEAD

    def cur(b, r, lb):
        return (b, lb, r)

    def prev(b, r, lb):
        return (b, jnp.maximum(lb - 1, 0), r)

    return pl.pallas_call(
        _attn_kernel,
        grid=(bsz, dil, nb),
        in_specs=[
            pl.BlockSpec((1, SPAN, qd), cur),
            pl.BlockSpec((1, SPAN, kd), prev),
            pl.BlockSpec((1, SPAN, kd), cur),
            pl.BlockSpec((1, SPAN, kd), prev),
            pl.BlockSpec((1, SPAN, kd), cur),
        ],
        out_specs=[pl.BlockSpec((1, SPAN, qd), cur), pl.BlockSpec((1, SPAN, HEAD), cur)],
        out_shape=[jax.ShapeDtypeStruct((bsz, sub, dil * qd), BF16),
                   jax.ShapeDtypeStruct((bsz, sub, dil * HEAD), F32)],
        compiler_params=_cparams("parallel", "parallel", "parallel"),
        name=f"dilated_attention_{gi}",
    )(q, k, k, v, v)


def kernel(x, c, positions, ada_w, ada_b, norm_g, mlp_w1, mlp_w2, gdn_w_in, gdn_conv_w, gdn_a_log,
           gdn_dt_bias, gdn_onorm_g, gdn_w_out, kv_norm_g, kv_ada_w, kv_ada_b, w_kv, k_norm_g,
           attn_w_q, q_norm_g, attn_w_o):
    bsz, seq, d = x.shape
    depth = ada_w.shape[0]
    n_a = gdn_w_in.shape[0]
    m = bsz * seq
    xf = x.reshape(m, d)

    mod = ada_modulation(c, ada_w, ada_b)
    kv_mod = ada_modulation(c, kv_ada_w[None], kv_ada_b[None])[0]
    cos_tab, sin_tab = rope_tables(positions)

    kv_cols = w_kv.reshape(d, N_GROUPS, 2, ATT_HKV * HEAD)
    w_kv_p = jnp.concatenate([_permute_head_lanes(kv_cols[:, :, 0]), kv_cols[:, :, 1]],
                             axis=-1).reshape(d, -1).astype(BF16)
    k_gain_p = _permute_head_lanes(k_norm_g)

    kv = None
    for layer in range(depth):
        sh1, sc1, gt1, sh2, sc2, gt2 = [mod[layer, :, i * d:(i + 1) * d] for i in range(6)]
        if layer < n_a:
            w_in = jnp.pad(gdn_w_in[layer], ((0, 0), (0, GDN_PROJ_PAD - gdn_w_in.shape[2])))
            proj = norm_mod_matmul(xf, norm_g[layer, 0], sh1, sc1, w_in.astype(BF16), seq,
                                   out_dtype=F32, tm=512, tn=GDN_PROJ_PAD // 5)
            gates, gates_t = gdn_gates(proj, gdn_a_log[layer], gdn_dt_bias[layer])
            o = gdn_core(proj, gdn_conv_w[layer], gates, gates_t, gdn_onorm_g[layer], bsz, seq)
            xf = matmul_gated_residual(o, gdn_w_out[layer].astype(BF16), xf, gt1, seq)
        else:
            if kv is None:
                kv = branch_projection(xf, kv_norm_g, kv_mod[:, :d], kv_mod[:, d:], w_kv_p,
                                       k_gain_p, cos_tab, sin_tab,
                                       [(ATT_HKV, True), (ATT_HKV, False)], bsz, seq)
            j = layer - n_a
            qs = branch_projection(xf, norm_g[layer, 0], sh1, sc1,
                                   _permute_head_lanes(attn_w_q[j]).astype(BF16),
                                   _permute_head_lanes(q_norm_g[j]), cos_tab, sin_tab,
                                   [(ATT_HQ, True)], bsz, seq)
            branches = [dilated_attention_branch(qs[gi], kv[2 * gi], kv[2 * gi + 1], gi)
                        for gi in range(N_GROUPS)]
            xf = merge_matmul_gated_residual([b[0] for b in branches], [b[1] for b in branches],
                                             attn_w_o[j].astype(BF16), xf, gt1, seq)
        xf = mlp_sublayer(xf, norm_g[layer, 1], sh2, sc2, gt2, mlp_w1[layer].astype(BF16),
                          mlp_w2[layer].astype(BF16), seq)
    return xf.reshape(bsz, seq, d)
```

```python
import functools

import numpy as np
import jax
import jax.numpy as jnp
from jax import lax
from jax.experimental import pallas as pl
from jax.experimental.pallas import tpu as pltpu

F32 = jnp.float32
BF16 = jnp.bfloat16

NORM_EPS = 1e-6
HEAD = 128
GDN_NK = 16
GDN_NV = 32
GDN_CONV_K = 4
CHUNK = 64
GDN_QK = GDN_NK * HEAD
GDN_V = GDN_NV * HEAD
GDN_CONV = 2 * GDN_QK + GDN_V
GDN_BA_COL = GDN_CONV + GDN_V
GDN_PROJ_PAD = GDN_BA_COL + 512
ATT_HQ = 16
ATT_HKV = 4
ATT_G = ATT_HQ // ATT_HKV
DILATIONS = (1, 4, 16)
SPAN = 128
N_GROUPS = 3
ROPE_THETA = 500000.0
ROT_DIM = HEAD // 4
NEG_BIG = -1e30

VMEM_LIMIT = 56 * 1024 * 1024


def _cparams(*sem):
    return pltpu.CompilerParams(dimension_semantics=sem, vmem_limit_bytes=VMEM_LIMIT)


def _silu(v):
    return v * jax.nn.sigmoid(v)


def _ada_kernel(c_ref, w_ref, b_ref, o_ref):
    c = c_ref[...]
    ca = _silu(c).astype(BF16)
    w = w_ref[0].astype(BF16)
    o_ref[0] = jnp.dot(ca, w, preferred_element_type=F32) + b_ref[0]


def ada_modulation(c, w, b, tn=1024):
    nl, d, n = w.shape
    bsz = c.shape[0]
    return pl.pallas_call(
        _ada_kernel,
        grid=(nl, n // tn),
        in_specs=[
            pl.BlockSpec((bsz, d), lambda l, j: (0, 0)),
            pl.BlockSpec((1, d, tn), lambda l, j: (l, 0, j)),
            pl.BlockSpec((1, 1, tn), lambda l, j: (l, 0, j)),
        ],
        out_specs=pl.BlockSpec((1, bsz, tn), lambda l, j: (l, 0, j)),
        out_shape=jax.ShapeDtypeStruct((nl, bsz, n), F32),
        compiler_params=_cparams("parallel", "parallel"),
        name="ada_modulation",
    )(c, w, b.reshape(nl, 1, n))


ROT_HALF = ROT_DIM // 2
HEAD_LANE_ORDER = np.concatenate([np.arange(0, ROT_HALF), np.arange(ROT_DIM, HEAD // 2 + ROT_HALF),
                                  np.arange(ROT_HALF, ROT_DIM),
                                  np.arange(HEAD // 2 + ROT_HALF, HEAD)])


def _permute_head_lanes(a):
    lead = a.shape[:-1]
    a = a.reshape(lead + (a.shape[-1] // HEAD, HEAD))
    parts = [a[..., 0:ROT_HALF], a[..., ROT_DIM:HEAD // 2 + ROT_HALF], a[..., ROT_HALF:ROT_DIM],
             a[..., HEAD // 2 + ROT_HALF:]]
    return jnp.concatenate(parts, axis=-1).reshape(lead + (-1,))


def _rope_kernel(pos_ref, inv_ref, cos_ref, sin_ref):
    ang = pos_ref[...].astype(F32) * inv_ref[...]
    lane = lax.broadcasted_iota(jnp.int32, ang.shape, 1)
    c = jnp.cos(ang)
    s = jnp.sin(ang)
    first = lane < ROT_HALF
    second = (lane >= HEAD // 2) & (lane < HEAD // 2 + ROT_HALF)
    cos_ref[...] = jnp.where(first | second, c, 1.0)
    sin_ref[...] = jnp.where(first, -s, jnp.where(second, s, 0.0))


def rope_tables(positions, ts=2048):
    m = positions.size
    inv = ROPE_THETA ** (-np.arange(0, ROT_DIM, 2, dtype=np.float32) / ROT_DIM)
    inv_row = np.zeros((1, HEAD), np.float32)
    inv_row[0, :ROT_HALF] = inv
    inv_row[0, HEAD // 2:HEAD // 2 + ROT_HALF] = inv
    ts = min(ts, m)
    return pl.pallas_call(
        _rope_kernel,
        grid=(m // ts,),
        in_specs=[pl.BlockSpec((ts, 1), lambda i: (i, 0)),
                  pl.BlockSpec((1, HEAD), lambda i: (0, 0))],
        out_specs=[pl.BlockSpec((ts, HEAD), lambda i: (i, 0))] * 2,
        out_shape=[jax.ShapeDtypeStruct((m, HEAD), F32)] * 2,
        compiler_params=_cparams("parallel"),
        name="rope_tables",
    )(positions.reshape(m, 1), jnp.asarray(inv_row))


def _head_norm_rope(t, gain, cos_t, sin_t):
    y = t * lax.rsqrt(jnp.mean(t * t, axis=-1, keepdims=True) + NORM_EPS) * gain
    return y * cos_t + pltpu.roll(y, HEAD // 2, 1) * sin_t


def _norm_modulate(x, gain, shift, scale):
    y = x * lax.rsqrt(jnp.mean(x * x, axis=-1, keepdims=True) + NORM_EPS) * gain
    return y * (1.0 + scale) + shift


def _nmm_kernel(x_ref, sh_ref, sc_ref, g_ref, w_ref, o_ref, h_ref):
    @pl.when(pl.program_id(1) == 0)
    def _():
        h_ref[...] = _norm_modulate(x_ref[...], g_ref[...], sh_ref[0], sc_ref[0]).astype(BF16)

    o_ref[...] = jnp.dot(h_ref[...], w_ref[...], preferred_element_type=F32).astype(o_ref.dtype)


def norm_mod_matmul(x, gain, shift, scale, w, seq, *, out_dtype, tm=1024, tn=512):
    m, d = x.shape
    n = w.shape[1]
    bsz = shift.shape[0]
    tpb = seq // tm
    return pl.pallas_call(
        _nmm_kernel,
        grid=(m // tm, n // tn),
        in_specs=[
            pl.BlockSpec((tm, d), lambda i, j: (i, 0)),
            pl.BlockSpec((1, 1, d), lambda i, j: (i // tpb, 0, 0)),
            pl.BlockSpec((1, 1, d), lambda i, j: (i // tpb, 0, 0)),
            pl.BlockSpec((1, d), lambda i, j: (0, 0)),
            pl.BlockSpec((d, tn), lambda i, j: (0, j)),
        ],
        out_specs=pl.BlockSpec((tm, tn), lambda i, j: (i, j)),
        out_shape=jax.ShapeDtypeStruct((m, n), out_dtype),
        scratch_shapes=[pltpu.VMEM((tm, d), BF16)],
        compiler_params=_cparams("parallel", "arbitrary"),
        name="norm_mod_matmul",
    )(x, shift.reshape(bsz, 1, d), scale.reshape(bsz, 1, d), gain.reshape(1, d), w)


def _dilated_rows(r, n, dil):
    return pl.ds(r, n, stride=dil) if dil > 1 else pl.ds(0, n)


def _residue_major_perm(tm, dil):
    n = tm // dil
    j = np.arange(tm)
    p = np.zeros((tm, tm), np.float32)
    p[j, (j % n) * dil + j // n] = 1.0
    return p


def _branch_proj_kernel(x_ref, sh_ref, sc_ref, g_ref, w_ref, cos_ref, sin_ref, hg_ref, p1_ref,
                        p2_ref, *rest, segments):
    n_out = len(segments) * N_GROUPS
    outs = rest[:n_out]
    h_ref, y_ref = rest[n_out:]
    perms = (None, p1_ref, p2_ref)
    grp = pl.program_id(1)
    tm = x_ref.shape[0]

    @pl.when(grp == 0)
    def _():
        h_ref[...] = _norm_modulate(x_ref[...], g_ref[...], sh_ref[0], sc_ref[0]).astype(BF16)

    acc = jnp.dot(h_ref[...], w_ref[...], preferred_element_type=F32)
    gain = hg_ref[0]
    cos_t = cos_ref[...]
    sin_t = sin_ref[...]
    ri = lax.broadcasted_iota(jnp.int32, (2 * HEAD, 2 * HEAD), 0) // HEAD
    ci = lax.broadcasted_iota(jnp.int32, (2 * HEAD, 2 * HEAD), 1) // HEAD
    ones_bd = jnp.where(ri == ci, 1.0, 0.0).astype(BF16)
    gain2 = jnp.concatenate([gain, gain], axis=-1)
    cos2 = jnp.concatenate([cos_t, cos_t], axis=-1)
    sin2 = jnp.concatenate([sin_t, sin_t], axis=-1)
    hh = 0
    for heads, roped in segments:
        for _ in range(heads // 2):
            cols = slice(hh * HEAD, (hh + 2) * HEAD)
            t = acc[:, cols]
            if roped:
                ss = jnp.dot((t * t).astype(BF16), ones_bd, preferred_element_type=F32)
                y = t * lax.rsqrt(ss * (1.0 / HEAD) + NORM_EPS) * gain2
                swapped = jnp.concatenate([pltpu.roll(y[:, :HEAD], HEAD // 2, 1),
                                           pltpu.roll(y[:, HEAD:], HEAD // 2, 1)], axis=-1)
                t = y * cos2 + swapped * sin2
            y_ref[:, cols] = t.astype(BF16)
            hh += 2

    def write_branch(gi):
        dil = DILATIONS[gi]
        n = tm // dil
        y = y_ref[...]
        if dil > 1:
            y = jnp.dot(perms[gi][...], y, preferred_element_type=F32).astype(BF16)
        c0 = 0
        for si, (heads, _) in enumerate(segments):
            o_ref = outs[gi * len(segments) + si]
            width = heads * HEAD
            for r in range(dil):
                o_ref[0, :, r * width:(r + 1) * width] = y[r * n:(r + 1) * n, c0:c0 + width]
            c0 += width

    for gi in range(N_GROUPS):
        pl.when(grp == gi)(functools.partial(write_branch, gi))


def branch_projection(x, gain, shift, scale, w, head_gain, cos_tab, sin_tab, segments, bsz, seq,
                      *, tm=512):
    m, d = x.shape
    cgrp = w.shape[1] // N_GROUPS
    tpb = seq // tm
    out_specs, out_shapes = [], []
    for gi in range(N_GROUPS):
        dil = DILATIONS[gi]
        for heads, _ in segments:
            width = dil * heads * HEAD
            out_specs.append(pl.BlockSpec((1, tm // dil, width),
                                          lambda i, g: (i // tpb, i % tpb, 0)))
            out_shapes.append(jax.ShapeDtypeStruct((bsz, seq // dil, width), BF16))
    return pl.pallas_call(
        functools.partial(_branch_proj_kernel, segments=tuple(segments)),
        grid=(m // tm, N_GROUPS),
        in_specs=[
            pl.BlockSpec((tm, d), lambda i, g: (i, 0)),
            pl.BlockSpec((1, 1, d), lambda i, g: (i // tpb, 0, 0)),
            pl.BlockSpec((1, 1, d), lambda i, g: (i // tpb, 0, 0)),
            pl.BlockSpec((1, d), lambda i, g: (0, 0)),
            pl.BlockSpec((d, cgrp), lambda i, g: (0, g)),
            pl.BlockSpec((tm, HEAD), lambda i, g: (i, 0)),
            pl.BlockSpec((tm, HEAD), lambda i, g: (i, 0)),
            pl.BlockSpec((1, 1, HEAD), lambda i, g: (g, 0, 0)),
            pl.BlockSpec((tm, tm), lambda i, g: (0, 0)),
            pl.BlockSpec((tm, tm), lambda i, g: (0, 0)),
        ],
        out_specs=out_specs,
        out_shape=out_shapes,
        scratch_shapes=[pltpu.VMEM((tm, d), BF16), pltpu.VMEM((tm, cgrp), BF16)],
        compiler_params=_cparams("parallel", "arbitrary"),
        name="branch_projection",
    )(x, shift.reshape(bsz, 1, d), scale.reshape(bsz, 1, d), gain.reshape(1, d), w,
      cos_tab, sin_tab, head_gain.reshape(N_GROUPS, 1, HEAD),
      jnp.asarray(_residue_major_perm(tm, DILATIONS[1]), BF16),
      jnp.asarray(_residue_major_perm(tm, DILATIONS[2]), BF16))


def _mm_res_kernel(a_ref, w_ref, x_ref, gt_ref, o_ref):
    y = jnp.dot(a_ref[...], w_ref[...], preferred_element_type=F32)
    o_ref[...] = x_ref[...] + gt_ref[0] * y


def matmul_gated_residual(a, w, x, gate, seq, *, tm=1024, tn=512):
    m, k = a.shape
    d = w.shape[1]
    bsz = gate.shape[0]
    tpb = seq // tm
    return pl.pallas_call(
        _mm_res_kernel,
        grid=(m // tm, d // tn),
        in_specs=[
            pl.BlockSpec((tm, k), lambda i, j: (i, 0)),
            pl.BlockSpec((k, tn), lambda i, j: (0, j)),
            pl.BlockSpec((tm, tn), lambda i, j: (i, j)),
            pl.BlockSpec((1, 1, tn), lambda i, j: (i // tpb, 0, j)),
        ],
        out_specs=pl.BlockSpec((tm, tn), lambda i, j: (i, j)),
        out_shape=jax.ShapeDtypeStruct((m, d), F32),
        compiler_params=_cparams("parallel", "parallel"),
        name="matmul_gated_residual",
    )(a, w, x, gate.reshape(bsz, 1, d))


def _merge_mm_res_kernel(o0_ref, o1_ref, o2_ref, l0_ref, l1_ref, l2_ref, w_ref, x_ref, gt_ref,
                         p1_ref, p2_ref, o_ref, a_ref, stack_ref, onat_ref, l3_ref):
    j = pl.program_id(1)
    tm = x_ref.shape[0]
    o_refs = (o0_ref, o1_ref, o2_ref)
    l_refs = (l0_ref, l1_ref, l2_ref)
    perms = (None, p1_ref, p2_ref)

    @pl.when(j == 0)
    def _():
        for gi, dil in enumerate(DILATIONS):
            n = tm // dil
            for r in range(dil):
                l3_ref[gi, _dilated_rows(r, n, dil), :] = l_refs[gi][0, :, r * HEAD:(r + 1) * HEAD]
        l0, l1, l2 = l3_ref[0], l3_ref[1], l3_ref[2]
        mx = jnp.maximum(jnp.maximum(l0, l1), l2)
        e0 = jnp.exp(l0 - mx)
        e1 = jnp.exp(l1 - mx)
        e2 = jnp.exp(l2 - mx)
        den = e0 + e1 + e2
        l3_ref[0] = e0 / den
        l3_ref[1] = e1 / den
        l3_ref[2] = e2 / den
        kdim = a_ref.shape[1]
        for gi in range(1, N_GROUPS):
            dil = DILATIONS[gi]
            n = tm // dil
            for r in range(dil):
                stack_ref[r * n:(r + 1) * n, :] = o_refs[gi][0, :, r * kdim:(r + 1) * kdim]
            onat_ref[gi - 1] = jnp.dot(perms[gi][...], stack_ref[...],
                                       preferred_element_type=F32).astype(BF16)
        eh = lax.broadcasted_iota(jnp.int32, (2 * HEAD, kdim), 0) & (HEAD - 1)
        ec = lax.broadcasted_iota(jnp.int32, (2 * HEAD, kdim), 1) // HEAD
        expand = jnp.where(eh == ec, 1.0, 0.0).astype(BF16)

        def spread(wts):
            hi = wts.astype(BF16)
            lo = (wts - hi.astype(F32)).astype(BF16)
            return jnp.dot(jnp.concatenate([hi, lo], axis=-1), expand,
                           preferred_element_type=F32)

        merged = spread(l3_ref[0]) * o0_ref[0].astype(F32)
        for gi in range(1, N_GROUPS):
            merged = merged + spread(l3_ref[gi]) * onat_ref[gi - 1].astype(F32)
        a_ref[...] = merged.astype(BF16)

    y = jnp.dot(a_ref[...], w_ref[...], preferred_element_type=F32)
    o_ref[...] = x_ref[...] + gt_ref[0] * y


def merge_matmul_gated_residual(outs, lses, w, x, gate, seq, *, tm=512, tn=1024):
    m, d = x.shape
    k = w.shape[0]
    bsz = gate.shape[0]
    tpb = seq // tm
    o_specs = [pl.BlockSpec((1, tm // dil, dil * k), lambda i, j: (i // tpb, i % tpb, 0))
               for dil in DILATIONS]
    l_specs = [pl.BlockSpec((1, tm // dil, dil * HEAD), lambda i, j: (i // tpb, i % tpb, 0))
               for dil in DILATIONS]
    return pl.pallas_call(
        _merge_mm_res_kernel,
        grid=(m // tm, d // tn),
        in_specs=o_specs + l_specs + [
            pl.BlockSpec((k, tn), lambda i, j: (0, j)),
            pl.BlockSpec((tm, tn), lambda i, j: (i, j)),
            pl.BlockSpec((1, 1, tn), lambda i, j: (i // tpb, 0, j)),
            pl.BlockSpec((tm, tm), lambda i, j: (0, 0)),
            pl.BlockSpec((tm, tm), lambda i, j: (0, 0))],
        out_specs=pl.BlockSpec((tm, tn), lambda i, j: (i, j)),
        out_shape=jax.ShapeDtypeStruct((m, d), F32),
        scratch_shapes=[pltpu.VMEM((tm, k), BF16), pltpu.VMEM((tm, k), BF16),
                        pltpu.VMEM((N_GROUPS - 1, tm, k), BF16),
                        pltpu.VMEM((N_GROUPS, tm, HEAD), F32)],
        compiler_params=_cparams("parallel", "arbitrary"),
        name="merge_matmul_gated_residual",
    )(*outs, *lses, w, x, gate.reshape(bsz, 1, d),
      jnp.asarray(_residue_major_perm(tm, DILATIONS[1]).T, BF16),
      jnp.asarray(_residue_major_perm(tm, DILATIONS[2]).T, BF16))


def _mlp_kernel(x_ref, sh_ref, sc_ref, gt_ref, g_ref, w1_ref, w2_ref, o_ref, h_ref, acc_ref,
                a_ref, *, nf):
    f = pl.program_id(1)

    def up():
        a = jnp.dot(h_ref[...], w1_ref[...], preferred_element_type=F32)
        return jnp.square(jnp.maximum(a, 0.0)).astype(BF16)

    def down(slot):
        acc_ref[...] += jnp.dot(a_ref[slot], w2_ref[...], preferred_element_type=F32)

    @pl.when(f == 0)
    def _():
        h_ref[...] = _norm_modulate(x_ref[...], g_ref[...], sh_ref[0], sc_ref[0]).astype(BF16)
        acc_ref[...] = jnp.zeros_like(acc_ref)
        a_ref[0] = up()

    for slot in range(2):
        @pl.when((f > 0) & (f < nf) & (f % 2 == slot))
        def _(slot=slot):
            a_new = up()
            down(1 - slot)
            a_ref[slot] = a_new

    @pl.when(f == nf)
    def _():
        down((nf - 1) % 2)
        o_ref[...] = x_ref[...] + gt_ref[0] * acc_ref[...]


def mlp_sublayer(x, gain, shift, scale, gate, w1, w2, seq, *, tm=512, tf=1024):
    m, d = x.shape
    dff = w1.shape[1]
    nf = dff // tf
    bsz = gate.shape[0]
    tpb = seq // tm
    vec = pl.BlockSpec((1, 1, d), lambda i, f: (i // tpb, 0, 0))
    return pl.pallas_call(
        functools.partial(_mlp_kernel, nf=nf),
        grid=(m // tm, nf + 1),
        in_specs=[
            pl.BlockSpec((tm, d), lambda i, f: (i, 0)),
            vec, vec, vec,
            pl.BlockSpec((1, d), lambda i, f: (0, 0)),
            pl.BlockSpec((d, tf), lambda i, f: (0, jnp.minimum(f, nf - 1))),
            pl.BlockSpec((tf, d), lambda i, f: (jnp.maximum(f - 1, 0), 0)),
        ],
        out_specs=pl.BlockSpec((tm, d), lambda i, f: (i, 0)),
        out_shape=jax.ShapeDtypeStruct((m, d), F32),
        scratch_shapes=[pltpu.VMEM((tm, d), BF16), pltpu.VMEM((tm, d), F32),
                        pltpu.VMEM((2, tm, tf), BF16)],
        compiler_params=_cparams("parallel", "arbitrary"),
        name="mlp_sublayer",
    )(x, shift.reshape(bsz, 1, d), scale.reshape(bsz, 1, d), gate.reshape(bsz, 1, d),
      gain.reshape(1, d), w1, w2)


def _softplus(v):
    return jnp.maximum(v, 0.0) + jnp.log1p(jnp.exp(-jnp.abs(v)))


def _gates_kernel(ba_ref, alog_ref, dtb_ref, tril_ref, g_ref, gt_ref):
    ba = ba_ref[...]
    lane = lax.broadcasted_iota(jnp.int32, ba.shape, 1)
    is_alpha = (lane >= GDN_NV) & (lane < 2 * GDN_NV)
    g = jnp.where(is_alpha, -jnp.exp(alog_ref[...]) * _softplus(ba + dtb_ref[...]), 0.0)
    gc = jnp.dot(tril_ref[...], g, preferred_element_type=F32, precision=lax.Precision.HIGHEST)
    out = jnp.where(lane < GDN_NV, jax.nn.sigmoid(ba), gc)
    g_ref[...] = out
    gt_ref[...] = out.T


def gdn_gates(proj, a_log, dt_bias, tg=512):
    m = proj.shape[0]
    tg = min(tg, m)
    alog_row = jnp.zeros((1, HEAD), F32).at[0, GDN_NV:2 * GDN_NV].set(a_log)
    dtb_row = jnp.zeros((1, HEAD), F32).at[0, GDN_NV:2 * GDN_NV].set(dt_bias)
    idx = np.arange(tg)
    tril = ((idx[:, None] >= idx[None, :]) & (idx[:, None] // CHUNK == idx[None, :] // CHUNK))
    return pl.pallas_call(
        _gates_kernel,
        grid=(m // tg,),
        in_specs=[pl.BlockSpec((tg, HEAD), lambda i: (i, GDN_BA_COL // HEAD)),
                  pl.BlockSpec((1, HEAD), lambda i: (0, 0)),
                  pl.BlockSpec((1, HEAD), lambda i: (0, 0)),
                  pl.BlockSpec((tg, tg), lambda i: (0, 0))],
        out_specs=[pl.BlockSpec((tg, HEAD), lambda i: (i, 0)),
                   pl.BlockSpec((HEAD, tg), lambda i: (0, i))],
        out_shape=[jax.ShapeDtypeStruct((m, HEAD), F32), jax.ShapeDtypeStruct((HEAD, m), F32)],
        compiler_params=_cparams("parallel"),
        name="gdn_gates",
    )(proj, alog_row, dtb_row, jnp.asarray(tril.astype(np.float32)))


def _dot_t(a, b):
    return lax.dot_general(a, b, (((1,), (1,)), ((), ())), preferred_element_type=F32)


def _tdot(a, b):
    return lax.dot_general(a, b, (((0,), (0,)), ((), ())), preferred_element_type=F32)


def _bmm(a, b):
    return jnp.einsum("bij,bjk->bik", a.astype(BF16), b.astype(BF16),
                      preferred_element_type=F32)


def _bmm_t(a, b):
    return jnp.einsum("bik,bjk->bij", a.astype(BF16), b.astype(BF16),
                      preferred_element_type=F32)


def _pair_blockdiag(m2):
    first = lax.broadcasted_iota(jnp.int32, m2.shape[1:], 1) < CHUNK
    return jnp.concatenate([jnp.where(first, m2, 0.0), jnp.where(first, 0.0, m2)], axis=1)


def _unit_lower_inverse_pairs(l2):
    i = lax.broadcasted_iota(jnp.int32, (CHUNK, 2 * CHUNK), 0)
    j = lax.broadcasted_iota(jnp.int32, (CHUNK, 2 * CHUNK), 1) & (CHUNK - 1)

    def sub_blocks(s):
        return (((i >> s) & 1) == 1) & ((j >> s) == (i >> s) - 1)

    inv = jnp.where(i == j, 1.0, 0.0) - jnp.where(sub_blocks(0), l2, 0.0)
    s = 1
    while (1 << s) < CHUNK:
        x = _bmm(inv, _pair_blockdiag(jnp.where(sub_blocks(s), l2, 0.0)))
        inv = inv - _bmm(x, _pair_blockdiag(inv))
        s += 1
    return inv


def _gdn_kernel(q_ref, k_ref, v_ref, z_ref, wq_ref, wk_ref, wv_ref, g_ref, gt_ref, og_ref,
                o_ref, qbuf, kbuf, vbuf, state_ref, *, tt, hg):
    hgi = pl.program_id(1)
    t = pl.program_id(2)
    nc = tt // CHUNK
    nh = 2 * hg

    @pl.when(t == 0)
    def _():
        state_ref[...] = jnp.zeros_like(state_ref)
        qbuf[pl.ds(tt, 8), :] = jnp.zeros((8, hg * HEAD), F32)
        kbuf[pl.ds(tt, 8), :] = jnp.zeros((8, hg * HEAD), F32)
        vbuf[pl.ds(tt, 8), :] = jnp.zeros((8, nh * HEAD), F32)

    def conv_silu(x_ref, buf, w_ref):
        buf[pl.ds(0, 8), :] = buf[pl.ds(tt, 8), :]
        buf[pl.ds(8, tt), :] = x_ref[0]
        acc = w_ref[GDN_CONV_K - 1:GDN_CONV_K, :] * buf[pl.ds(8, tt), :]
        for j in range(1, GDN_CONV_K):
            tap = GDN_CONV_K - 1 - j
            acc = acc + w_ref[tap:tap + 1, :] * buf[pl.ds(8 - j, tt), :]
        return _silu(acc)

    def l2n(v):
        return v * lax.rsqrt(jnp.sum(v * v, axis=-1, keepdims=True) + NORM_EPS)

    def chunked(v):
        return v.reshape(nc, CHUNK, HEAD)

    qa = conv_silu(q_ref, qbuf, wq_ref)
    ka = conv_silu(k_ref, kbuf, wk_ref)
    va = conv_silu(v_ref, vbuf, wv_ref)

    gates = g_ref[0]
    lane = lax.broadcasted_iota(jnp.int32, gates.shape, 1)
    first_tt = lane < CHUNK
    row_i = lax.broadcasted_iota(jnp.int32, (CHUNK, 2 * CHUNK), 0)
    col_j = lax.broadcasted_iota(jnp.int32, (CHUNK, 2 * CHUNK), 1) & (CHUNK - 1)
    causal = row_i >= col_j
    strict = row_i > col_j
    zeros_rhs = jnp.zeros((nc, CHUNK, 2 * HEAD), F32)

    lmats, rhss, qgs, kgs, pmats, elasts = [], [], [], [], [], []
    for khl in range(hg):
        hs = slice(khl * HEAD, (khl + 1) * HEAD)
        qc = chunked(l2n(qa[:, hs]) * (HEAD ** -0.5))
        kc = chunked(l2n(ka[:, hs]))
        kc2 = jnp.concatenate([kc, kc], axis=1)
        kk2 = _bmm_t(kc, kc2)
        qk2 = _bmm_t(qc, kc2)
        beta_cols, gc_cols, gc_rows = [], [], []
        for e in range(2):
            hv = hgi * nh + 2 * khl + e
            beta_cols.append(jnp.sum(jnp.where(lane == hv, gates, 0.0), axis=-1, keepdims=True))
            gc_cols.append(
                jnp.sum(jnp.where(lane == hv + GDN_NV, gates, 0.0), axis=-1, keepdims=True))
            gc_rows.append(gt_ref[pl.ds(hv + GDN_NV, 1), :])
        bi2 = jnp.where(first_tt, beta_cols[0], beta_cols[1]).reshape(nc, CHUNK, HEAD)
        gci2 = jnp.where(first_tt, gc_cols[0], gc_cols[1]).reshape(nc, CHUNK, HEAD)
        gcj2 = jnp.stack(
            [jnp.concatenate([gc_rows[0][:, c * CHUNK:(c + 1) * CHUNK],
                              gc_rows[1][:, c * CHUNK:(c + 1) * CHUNK]], axis=1)
             for c in range(nc)], axis=0)
        dec2 = jnp.exp(jnp.where(causal, gci2 - gcj2, NEG_BIG))
        lmats.append(jnp.where(strict, kk2 * bi2 * dec2, 0.0))
        pmats.append((qk2 * dec2).astype(BF16))
        rhs_pair = []
        for e in range(2):
            hl = 2 * khl + e
            bi = beta_cols[e].reshape(nc, CHUNK, 1)
            gci = gc_cols[e].reshape(nc, CHUNK, 1)
            egc = jnp.exp(gci)
            g_last = gci[:, CHUNK - 1:CHUNK, :]
            vc = chunked(va[:, hl * HEAD:(hl + 1) * HEAD])
            rhs_pair.append(jnp.concatenate([vc * bi, kc * (bi * egc)], axis=-1))
            qgs.append((qc * egc).astype(BF16))
            kgs.append((kc * jnp.exp(g_last - gci)).astype(BF16))
            elasts.append(jnp.exp(g_last))
        rhss.append(jnp.concatenate(
            [jnp.concatenate([rhs_pair[0], zeros_rhs], axis=-1),
             jnp.concatenate([zeros_rhs, rhs_pair[1]], axis=-1)], axis=1))
    inv2 = _unit_lower_inverse_pairs(jnp.concatenate(lmats, axis=0))
    sol2 = _bmm(inv2, jnp.concatenate(rhss, axis=0))

    states = [state_ref[hl] for hl in range(nh)]
    o_chunks = [[] for _ in range(nh)]
    zeros_vn = jnp.zeros((CHUNK, HEAD), BF16)
    for c in range(nc):
        for khl in range(hg):
            p = khl * nc + c
            qss, vnbs = [], []
            for e in range(2):
                hl = 2 * khl + e
                u = sol2[p, :, 2 * e * HEAD:(2 * e + 1) * HEAD]
                w = sol2[p, :, (2 * e + 1) * HEAD:(2 * e + 2) * HEAD]
                ws_qs = jnp.dot(jnp.concatenate([w.astype(BF16), qgs[hl][c]], axis=0),
                                states[hl].astype(BF16), preferred_element_type=F32)
                vnb = (u - ws_qs[:CHUNK]).astype(BF16)
                states[hl] = states[hl] * elasts[hl][c] + _tdot(kgs[hl][c], vnb)
                qss.append(ws_qs[CHUNK:])
                vnbs.append(vnb)
            vn_bd = jnp.concatenate(
                [jnp.concatenate([vnbs[0], zeros_vn], axis=-1),
                 jnp.concatenate([zeros_vn, vnbs[1]], axis=-1)], axis=0)
            pv = jnp.dot(pmats[khl][c], vn_bd, preferred_element_type=F32)
            for e in range(2):
                o_chunks[2 * khl + e].append(qss[e] + pv[:, e * HEAD:(e + 1) * HEAD])
    for hl in range(nh):
        state_ref[hl] = states[hl]
        o = jnp.concatenate(o_chunks[hl], axis=0)
        z = z_ref[0, :, hl * HEAD:(hl + 1) * HEAD]
        y = o * lax.rsqrt(jnp.mean(o * o, axis=-1, keepdims=True) + NORM_EPS) * og_ref[...]
        o_ref[0, :, hl * HEAD:(hl + 1) * HEAD] = (y * _silu(z)).astype(BF16)


def gdn_core(proj, conv_w, gates, gates_t, onorm_g, bsz, seq, *, tt=512, hg=4):
    tt = min(tt, seq)
    nt = seq // tt
    nh = 2 * hg
    proj3 = proj.reshape(bsz, seq, proj.shape[1])
    qw = hg * HEAD
    vw = nh * HEAD
    kq = GDN_QK // qw
    kv = 2 * GDN_QK // vw
    kz = GDN_CONV // vw
    out = pl.pallas_call(
        functools.partial(_gdn_kernel, tt=tt, hg=hg),
        grid=(bsz, GDN_NK // hg, nt),
        in_specs=[
            pl.BlockSpec((1, tt, qw), lambda b, h, t: (b, t, h)),
            pl.BlockSpec((1, tt, qw), lambda b, h, t: (b, t, kq + h)),
            pl.BlockSpec((1, tt, vw), lambda b, h, t: (b, t, kv + h)),
            pl.BlockSpec((1, tt, vw), lambda b, h, t: (b, t, kz + h)),
            pl.BlockSpec((GDN_CONV_K, qw), lambda b, h, t: (0, h)),
            pl.BlockSpec((GDN_CONV_K, qw), lambda b, h, t: (0, kq + h)),
            pl.BlockSpec((GDN_CONV_K, vw), lambda b, h, t: (0, kv + h)),
            pl.BlockSpec((1, tt, HEAD), lambda b, h, t: (b, t, 0)),
            pl.BlockSpec((HEAD, tt), lambda b, h, t: (0, b * nt + t)),
            pl.BlockSpec((1, HEAD), lambda b, h, t: (0, 0)),
        ],
        out_specs=pl.BlockSpec((1, tt, vw), lambda b, h, t: (b, t, h)),
        out_shape=jax.ShapeDtypeStruct((bsz, seq, GDN_V), BF16),
        scratch_shapes=[pltpu.VMEM((tt + 8, qw), F32), pltpu.VMEM((tt + 8, qw), F32),
                        pltpu.VMEM((tt + 8, vw), F32), pltpu.VMEM((nh, HEAD, HEAD), F32)],
        compiler_params=_cparams("parallel", "parallel", "arbitrary"),
        name="gdn_core",
    )(proj3, proj3, proj3, proj3, conv_w, conv_w, conv_w,
      gates.reshape(bsz, seq, HEAD), gates_t, onorm_g.reshape(1, HEAD))
    return out.reshape(bsz * seq, GDN_V)


def _attn_kernel(q_ref, kp_ref, kc_ref, vp_ref, vc_ref, o_ref, lse_ref, *, nsb):
    step = pl.program_id(2)
    rows = ATT_G * SPAN
    qi = lax.broadcasted_iota(jnp.int32, (rows, 2 * SPAN), 0) & (SPAN - 1)
    kj = lax.broadcasted_iota(jnp.int32, (rows, 2 * SPAN), 1)
    rel = qi + SPAN - kj
    band = (rel >= 0) & (rel <= SPAN)
    first_mask = band & (kj >= jnp.where(step > 0, 0, SPAN))
    scale = HEAD ** -0.5
    lse_ref[0] = jnp.zeros(lse_ref.shape[1:], F32)
    for sb in range(nsb):
        qrows = slice(sb * SPAN, (sb + 1) * SPAN)
        mask = first_mask if sb == 0 else band
        for h in range(ATT_HKV):
            hs = slice(h * HEAD, (h + 1) * HEAD)
            if sb == 0:
                kprev, vprev = kp_ref[0, :, hs], vp_ref[0, :, hs]
            else:
                kprev = kc_ref[0, (sb - 1) * SPAN:sb * SPAN, hs]
                vprev = vc_ref[0, (sb - 1) * SPAN:sb * SPAN, hs]
            k2 = jnp.concatenate([kprev, kc_ref[0, qrows, hs]], axis=0)
            v2 = jnp.concatenate([vprev, vc_ref[0, qrows, hs]], axis=0)
            q4 = jnp.concatenate(
                [q_ref[0, qrows, (h * ATT_G + g) * HEAD:(h * ATT_G + g + 1) * HEAD]
                 for g in range(ATT_G)], axis=0)
            s = jnp.where(mask, _dot_t(q4, k2) * scale, NEG_BIG)
            mx = jnp.max(s, axis=-1, keepdims=True)
            p = jnp.exp(s - mx)
            den = jnp.sum(p, axis=-1, keepdims=True)
            o4 = jnp.dot(p.astype(BF16), v2, preferred_element_type=F32) / den
            lse = mx + jnp.log(den)
            for g in range(ATT_G):
                hq = h * ATT_G + g
                o_ref[0, qrows, hq * HEAD:(hq + 1) * HEAD] = (
                    o4[g * SPAN:(g + 1) * SPAN].astype(o_ref.dtype))
                lse_ref[0, qrows, hq:hq + 1] = lse[g * SPAN:(g + 1) * SPAN]


def dilated_attention_branch(q, k, v, gi):
    dil = DILATIONS[gi]
    bsz, sub, _ = q.shape
    nsb = 2 if sub % (2 * SPAN) == 0 else 1
    tq = nsb * SPAN
    qd = ATT_HQ * HEAD
    kd = ATT_HKV * HEAD

    def cur(b, r, st):
        return (b, st, r)

    def prev(b, r, st):
        return (b, jnp.maximum(st * nsb - 1, 0), r)

    return pl.pallas_call(
        functools.partial(_attn_kernel, nsb=nsb),
        grid=(bsz, dil, sub // tq),
        in_specs=[
            pl.BlockSpec((1, tq, qd), cur),
            pl.BlockSpec((1, SPAN, kd), prev),
            pl.BlockSpec((1, tq, kd), cur),
            pl.BlockSpec((1, SPAN, kd), prev),
            pl.BlockSpec((1, tq, kd), cur),
        ],
        out_specs=[pl.BlockSpec((1, tq, qd), cur), pl.BlockSpec((1, tq, HEAD), cur)],
        out_shape=[jax.ShapeDtypeStruct((bsz, sub, dil * qd), BF16),
                   jax.ShapeDtypeStruct((bsz, sub, dil * HEAD), F32)],
        compiler_params=_cparams("parallel", "parallel", "parallel"),
        name=f"dilated_attention_{gi}",
    )(q, k, k, v, v)


def kernel(x, c, positions, ada_w, ada_b, norm_g, mlp_w1, mlp_w2, gdn_w_in, gdn_conv_w, gdn_a_log,
           gdn_dt_bias, gdn_onorm_g, gdn_w_out, kv_norm_g, kv_ada_w, kv_ada_b, w_kv, k_norm_g,
           attn_w_q, q_norm_g, attn_w_o):
    bsz, seq, d = x.shape
    depth = ada_w.shape[0]
    n_a = gdn_w_in.shape[0]
    m = bsz * seq
    xf = x.reshape(m, d)

    mod = ada_modulation(c, ada_w, ada_b)
    kv_mod = ada_modulation(c, kv_ada_w[None], kv_ada_b[None])[0]
    cos_tab, sin_tab = rope_tables(positions)

    kv_cols = w_kv.reshape(d, N_GROUPS, 2, ATT_HKV * HEAD)
    w_kv_p = jnp.concatenate([_permute_head_lanes(kv_cols[:, :, 0]), kv_cols[:, :, 1]],
                             axis=-1).reshape(d, -1).astype(BF16)
    k_gain_p = _permute_head_lanes(k_norm_g)

    kv = None
    for layer in range(depth):
        sh1, sc1, gt1, sh2, sc2, gt2 = [mod[layer, :, i * d:(i + 1) * d] for i in range(6)]
        if layer < n_a:
            w_in = jnp.pad(gdn_w_in[layer], ((0, 0), (0, GDN_PROJ_PAD - gdn_w_in.shape[2])))
            proj = norm_mod_matmul(xf, norm_g[layer, 0], sh1, sc1, w_in.astype(BF16), seq,
                                   out_dtype=F32, tm=512, tn=GDN_PROJ_PAD // 5)
            gates, gates_t = gdn_gates(proj, gdn_a_log[layer], gdn_dt_bias[layer])
            o = gdn_core(proj, gdn_conv_w[layer], gates, gates_t, gdn_onorm_g[layer], bsz, seq)
            xf = matmul_gated_residual(o, gdn_w_out[layer].astype(BF16), xf, gt1, seq)
        else:
            if kv is None:
                kv = branch_projection(xf, kv_norm_g, kv_mod[:, :d], kv_mod[:, d:], w_kv_p,
                                       k_gain_p, cos_tab, sin_tab,
                                       [(ATT_HKV, True), (ATT_HKV, False)], bsz, seq)
            j = layer - n_a
            qs = branch_projection(xf, norm_g[layer, 0], sh1, sc1,
                                   _permute_head_lanes(attn_w_q[j]).astype(BF16),
                                   _permute_head_lanes(q_norm_g[j]), cos_tab, sin_tab,
                                   [(ATT_HQ, True)], bsz, seq)
            branches = [dilated_attention_branch(qs[gi], kv[2 * gi], kv[2 * gi + 1], gi)
                        for gi in range(N_GROUPS)]
            xf = merge_matmul_gated_residual([b[0] for b in branches], [b[1] for b in branches],
                                             attn_w_o[j].astype(BF16), xf, gt1, seq)
        xf = mlp_sublayer(xf, norm_g[layer, 1], sh2, sc2, gt2, mlp_w1[layer].astype(BF16),
                          mlp_w2[layer].astype(BF16), seq)
    return xf.reshape(bsz, seq, d)
```

```python
import functools

import numpy as np
import jax
import jax.numpy as jnp
from jax import lax
from jax.experimental import pallas as pl
from jax.experimental.pallas import tpu as pltpu

F32 = jnp.float32
BF16 = jnp.bfloat16

NORM_EPS = 1e-6
HEAD = 128
GDN_NK = 16
GDN_NV = 32
GDN_CONV_K = 4
CHUNK = 64
GDN_QK = GDN_NK * HEAD
GDN_V = GDN_NV * HEAD
GDN_CONV = 2 * GDN_QK + GDN_V
GDN_BA_COL = GDN_CONV + GDN_V
GDN_PROJ_PAD = GDN_BA_COL + 512
ATT_HQ = 16
ATT_HKV = 4
ATT_G = ATT_HQ // ATT_HKV
DILATIONS = (1, 4, 16)
SPAN = 128
N_GROUPS = 3
ROPE_THETA = 500000.0
ROT_DIM = HEAD // 4
NEG_BIG = -1e30

VMEM_LIMIT = 56 * 1024 * 1024


def _cparams(*sem):
    return pltpu.CompilerParams(dimension_semantics=sem, vmem_limit_bytes=VMEM_LIMIT)


def _silu(v):
    return v * jax.nn.sigmoid(v)


def _ada_kernel(c_ref, w_ref, b_ref, o_ref):
    c = c_ref[...]
    ca = _silu(c).astype(BF16)
    w = w_ref[0].astype(BF16)
    o_ref[0] = jnp.dot(ca, w, preferred_element_type=F32) + b_ref[0]


def ada_modulation(c, w, b, tn=1024):
    nl, d, n = w.shape
    bsz = c.shape[0]
    return pl.pallas_call(
        _ada_kernel,
        grid=(nl, n // tn),
        in_specs=[
            pl.BlockSpec((bsz, d), lambda l, j: (0, 0)),
            pl.BlockSpec((1, d, tn), lambda l, j: (l, 0, j)),
            pl.BlockSpec((1, 1, tn), lambda l, j: (l, 0, j)),
        ],
        out_specs=pl.BlockSpec((1, bsz, tn), lambda l, j: (l, 0, j)),
        out_shape=jax.ShapeDtypeStruct((nl, bsz, n), F32),
        compiler_params=_cparams("parallel", "parallel"),
        name="ada_modulation",
    )(c, w, b.reshape(nl, 1, n))


ROT_HALF = ROT_DIM // 2
HEAD_LANE_ORDER = np.concatenate([np.arange(0, ROT_HALF), np.arange(ROT_DIM, HEAD // 2 + ROT_HALF),
                                  np.arange(ROT_HALF, ROT_DIM),
                                  np.arange(HEAD // 2 + ROT_HALF, HEAD)])


def _permute_head_lanes(a):
    lead = a.shape[:-1]
    a = a.reshape(lead + (a.shape[-1] // HEAD, HEAD))
    parts = [a[..., 0:ROT_HALF], a[..., ROT_DIM:HEAD // 2 + ROT_HALF], a[..., ROT_HALF:ROT_DIM],
             a[..., HEAD // 2 + ROT_HALF:]]
    return jnp.concatenate(parts, axis=-1).reshape(lead + (-1,))


def _rope_kernel(pos_ref, inv_ref, cos_ref, sin_ref):
    ang = pos_ref[...].astype(F32) * inv_ref[...]
    lane = lax.broadcasted_iota(jnp.int32, ang.shape, 1)
    c = jnp.cos(ang)
    s = jnp.sin(ang)
    first = lane < ROT_HALF
    second = (lane >= HEAD // 2) & (lane < HEAD // 2 + ROT_HALF)
    cos_ref[...] = jnp.where(first | second, c, 1.0)
    sin_ref[...] = jnp.where(first, -s, jnp.where(second, s, 0.0))


def rope_tables(positions, ts=2048):
    m = positions.size
    inv = ROPE_THETA ** (-np.arange(0, ROT_DIM, 2, dtype=np.float32) / ROT_DIM)
    inv_row = np.zeros((1, HEAD), np.float32)
    inv_row[0, :ROT_HALF] = inv
    inv_row[0, HEAD // 2:HEAD // 2 + ROT_HALF] = inv
    ts = min(ts, m)
    return pl.pallas_call(
        _rope_kernel,
        grid=(m // ts,),
        in_specs=[pl.BlockSpec((ts, 1), lambda i: (i, 0)),
                  pl.BlockSpec((1, HEAD), lambda i: (0, 0))],
        out_specs=[pl.BlockSpec((ts, HEAD), lambda i: (i, 0))] * 2,
        out_shape=[jax.ShapeDtypeStruct((m, HEAD), F32)] * 2,
        compiler_params=_cparams("parallel"),
        name="rope_tables",
    )(positions.reshape(m, 1), jnp.asarray(inv_row))


def _head_norm_rope(t, gain, cos_t, sin_t):
    y = t * lax.rsqrt(jnp.mean(t * t, axis=-1, keepdims=True) + NORM_EPS) * gain
    return y * cos_t + pltpu.roll(y, HEAD // 2, 1) * sin_t


def _norm_modulate(x, gain, shift, scale):
    y = x * lax.rsqrt(jnp.mean(x * x, axis=-1, keepdims=True) + NORM_EPS) * gain
    return y * (1.0 + scale) + shift


def _nmm_kernel(x_ref, sh_ref, sc_ref, g_ref, w_ref, o_ref, h_ref):
    @pl.when(pl.program_id(1) == 0)
    def _():
        h_ref[...] = _norm_modulate(x_ref[...], g_ref[...], sh_ref[0], sc_ref[0]).astype(BF16)

    o_ref[...] = jnp.dot(h_ref[...], w_ref[...], preferred_element_type=F32).astype(o_ref.dtype)


def norm_mod_matmul(x, gain, shift, scale, w, seq, *, out_dtype, tm=1024, tn=512):
    m, d = x.shape
    n = w.shape[1]
    bsz = shift.shape[0]
    tpb = seq // tm
    return pl.pallas_call(
        _nmm_kernel,
        grid=(m // tm, n // tn),
        in_specs=[
            pl.BlockSpec((tm, d), lambda i, j: (i, 0)),
            pl.BlockSpec((1, 1, d), lambda i, j: (i // tpb, 0, 0)),
            pl.BlockSpec((1, 1, d), lambda i, j: (i // tpb, 0, 0)),
            pl.BlockSpec((1, d), lambda i, j: (0, 0)),
            pl.BlockSpec((d, tn), lambda i, j: (0, j)),
        ],
        out_specs=pl.BlockSpec((tm, tn), lambda i, j: (i, j)),
        out_shape=jax.ShapeDtypeStruct((m, n), out_dtype),
        scratch_shapes=[pltpu.VMEM((tm, d), BF16)],
        compiler_params=_cparams("parallel", "arbitrary"),
        name="norm_mod_matmul",
    )(x, shift.reshape(bsz, 1, d), scale.reshape(bsz, 1, d), gain.reshape(1, d), w)


def _dilated_rows(r, n, dil):
    return pl.ds(r, n, stride=dil) if dil > 1 else pl.ds(0, n)


def _residue_major_perm(tm, dil):
    n = tm // dil
    j = np.arange(tm)
    p = np.zeros((tm, tm), np.float32)
    p[j, (j % n) * dil + j // n] = 1.0
    return p


def _branch_proj_kernel(x_ref, sh_ref, sc_ref, g_ref, w_ref, cos_ref, sin_ref, hg_ref, p1_ref,
                        p2_ref, *rest, segments):
    n_out = len(segments) * N_GROUPS
    outs = rest[:n_out]
    h_ref, y_ref = rest[n_out:]
    perms = (None, p1_ref, p2_ref)
    grp = pl.program_id(1)
    tm = x_ref.shape[0]

    @pl.when(grp == 0)
    def _():
        h_ref[...] = _norm_modulate(x_ref[...], g_ref[...], sh_ref[0], sc_ref[0]).astype(BF16)

    acc = jnp.dot(h_ref[...], w_ref[...], preferred_element_type=F32)
    gain = hg_ref[0]
    cos_t = cos_ref[...]
    sin_t = sin_ref[...]
    ri = lax.broadcasted_iota(jnp.int32, (2 * HEAD, 2 * HEAD), 0) // HEAD
    ci = lax.broadcasted_iota(jnp.int32, (2 * HEAD, 2 * HEAD), 1) // HEAD
    ones_bd = jnp.where(ri == ci, 1.0, 0.0).astype(BF16)
    gain2 = jnp.concatenate([gain, gain], axis=-1)
    cos2 = jnp.concatenate([cos_t, cos_t], axis=-1)
    sin2 = jnp.concatenate([sin_t, sin_t], axis=-1)
    hh = 0
    for heads, roped in segments:
        for _ in range(heads // 2):
            cols = slice(hh * HEAD, (hh + 2) * HEAD)
            t = acc[:, cols]
            if roped:
                ss = jnp.dot((t * t).astype(BF16), ones_bd, preferred_element_type=F32)
                y = t * lax.rsqrt(ss * (1.0 / HEAD) + NORM_EPS) * gain2
                swapped = jnp.concatenate([pltpu.roll(y[:, :HEAD], HEAD // 2, 1),
                                           pltpu.roll(y[:, HEAD:], HEAD // 2, 1)], axis=-1)
                t = y * cos2 + swapped * sin2
            y_ref[:, cols] = t.astype(BF16)
            hh += 2

    def write_branch(gi):
        dil = DILATIONS[gi]
        n = tm // dil
        y = y_ref[...]
        if dil > 1:
            y = jnp.dot(perms[gi][...], y, preferred_element_type=F32).astype(BF16)
        c0 = 0
        for si, (heads, _) in enumerate(segments):
            o_ref = outs[gi * len(segments) + si]
            width = heads * HEAD
            for r in range(dil):
                o_ref[0, :, r * width:(r + 1) * width] = y[r * n:(r + 1) * n, c0:c0 + width]
            c0 += width

    for gi in range(N_GROUPS):
        pl.when(grp == gi)(functools.partial(write_branch, gi))


def branch_projection(x, gain, shift, scale, w, head_gain, cos_tab, sin_tab, segments, bsz, seq,
                      *, tm=512):
    m, d = x.shape
    cgrp = w.shape[1] // N_GROUPS
    tpb = seq // tm
    out_specs, out_shapes = [], []
    for gi in range(N_GROUPS):
        dil = DILATIONS[gi]
        for heads, _ in segments:
            width = dil * heads * HEAD
            out_specs.append(pl.BlockSpec((1, tm // dil, width),
                                          lambda i, g: (i // tpb, i % tpb, 0)))
            out_shapes.append(jax.ShapeDtypeStruct((bsz, seq // dil, width), BF16))
    return pl.pallas_call(
        functools.partial(_branch_proj_kernel, segments=tuple(segments)),
        grid=(m // tm, N_GROUPS),
        in_specs=[
            pl.BlockSpec((tm, d), lambda i, g: (i, 0)),
            pl.BlockSpec((1, 1, d), lambda i, g: (i // tpb, 0, 0)),
            pl.BlockSpec((1, 1, d), lambda i, g: (i // tpb, 0, 0)),
            pl.BlockSpec((1, d), lambda i, g: (0, 0)),
            pl.BlockSpec((d, cgrp), lambda i, g: (0, g)),
            pl.BlockSpec((tm, HEAD), lambda i, g: (i, 0)),
            pl.BlockSpec((tm, HEAD), lambda i, g: (i, 0)),
            pl.BlockSpec((1, 1, HEAD), lambda i, g: (g, 0, 0)),
            pl.BlockSpec((tm, tm), lambda i, g: (0, 0)),
            pl.BlockSpec((tm, tm), lambda i, g: (0, 0)),
        ],
        out_specs=out_specs,
        out_shape=out_shapes,
        scratch_shapes=[pltpu.VMEM((tm, d), BF16), pltpu.VMEM((tm, cgrp), BF16)],
        compiler_params=_cparams("parallel", "arbitrary"),
        name="branch_projection",
    )(x, shift.reshape(bsz, 1, d), scale.reshape(bsz, 1, d), gain.reshape(1, d), w,
      cos_tab, sin_tab, head_gain.reshape(N_GROUPS, 1, HEAD),
      jnp.asarray(_residue_major_perm(tm, DILATIONS[1]), BF16),
      jnp.asarray(_residue_major_perm(tm, DILATIONS[2]), BF16))


def _mm_res_kernel(a_ref, w_ref, x_ref, gt_ref, o_ref):
    y = jnp.dot(a_ref[...], w_ref[...], preferred_element_type=F32)
    o_ref[...] = x_ref[...] + gt_ref[0] * y


def matmul_gated_residual(a, w, x, gate, seq, *, tm=1024, tn=512):
    m, k = a.shape
    d = w.shape[1]
    bsz = gate.shape[0]
    tpb = seq // tm
    return pl.pallas_call(
        _mm_res_kernel,
        grid=(m // tm, d // tn),
        in_specs=[
            pl.BlockSpec((tm, k), lambda i, j: (i, 0)),
            pl.BlockSpec((k, tn), lambda i, j: (0, j)),
            pl.BlockSpec((tm, tn), lambda i, j: (i, j)),
            pl.BlockSpec((1, 1, tn), lambda i, j: (i // tpb, 0, j)),
        ],
        out_specs=pl.BlockSpec((tm, tn), lambda i, j: (i, j)),
        out_shape=jax.ShapeDtypeStruct((m, d), F32),
        compiler_params=_cparams("parallel", "parallel"),
        name="matmul_gated_residual",
    )(a, w, x, gate.reshape(bsz, 1, d))


def _merge_mm_res_kernel(o0_ref, o1_ref, o2_ref, l0_ref, l1_ref, l2_ref, w_ref, x_ref, gt_ref,
                         p1_ref, p2_ref, o_ref, a_ref, stack_ref, onat_ref, l3_ref):
    j = pl.program_id(1)
    tm = x_ref.shape[0]
    o_refs = (o0_ref, o1_ref, o2_ref)
    l_refs = (l0_ref, l1_ref, l2_ref)
    perms = (None, p1_ref, p2_ref)

    @pl.when(j == 0)
    def _():
        for gi, dil in enumerate(DILATIONS):
            n = tm // dil
            for r in range(dil):
                l3_ref[gi, _dilated_rows(r, n, dil), :] = l_refs[gi][0, :, r * HEAD:(r + 1) * HEAD]
        l0, l1, l2 = l3_ref[0], l3_ref[1], l3_ref[2]
        mx = jnp.maximum(jnp.maximum(l0, l1), l2)
        e0 = jnp.exp(l0 - mx)
        e1 = jnp.exp(l1 - mx)
        e2 = jnp.exp(l2 - mx)
        den = e0 + e1 + e2
        l3_ref[0] = e0 / den
        l3_ref[1] = e1 / den
        l3_ref[2] = e2 / den
        kdim = a_ref.shape[1]
        for gi in range(1, N_GROUPS):
            dil = DILATIONS[gi]
            n = tm // dil
            for r in range(dil):
                stack_ref[r * n:(r + 1) * n, :] = o_refs[gi][0, :, r * kdim:(r + 1) * kdim]
            onat_ref[gi - 1] = jnp.dot(perms[gi][...], stack_ref[...],
                                       preferred_element_type=F32).astype(BF16)
        eh = lax.broadcasted_iota(jnp.int32, (2 * HEAD, kdim), 0) & (HEAD - 1)
        ec = lax.broadcasted_iota(jnp.int32, (2 * HEAD, kdim), 1) // HEAD
        expand = jnp.where(eh == ec, 1.0, 0.0).astype(BF16)

        def spread(wts):
            hi = wts.astype(BF16)
            lo = (wts - hi.astype(F32)).astype(BF16)
            return jnp.dot(jnp.concatenate([hi, lo], axis=-1), expand,
                           preferred_element_type=F32)

        merged = spread(l3_ref[0]) * o0_ref[0].astype(F32)
        for gi in range(1, N_GROUPS):
            merged = merged + spread(l3_ref[gi]) * onat_ref[gi - 1].astype(F32)
        a_ref[...] = merged.astype(BF16)

    y = jnp.dot(a_ref[...], w_ref[...], preferred_element_type=F32)
    o_ref[...] = x_ref[...] + gt_ref[0] * y


def merge_matmul_gated_residual(outs, lses, w, x, gate, seq, *, tm=512, tn=1024):
    m, d = x.shape
    k = w.shape[0]
    bsz = gate.shape[0]
    tpb = seq // tm
    o_specs = [pl.BlockSpec((1, tm // dil, dil * k), lambda i, j: (i // tpb, i % tpb, 0))
               for dil in DILATIONS]
    l_specs = [pl.BlockSpec((1, tm // dil, dil * HEAD), lambda i, j: (i // tpb, i % tpb, 0))
               for dil in DILATIONS]
    return pl.pallas_call(
        _merge_mm_res_kernel,
        grid=(m // tm, d // tn),
        in_specs=o_specs + l_specs + [
            pl.BlockSpec((k, tn), lambda i, j: (0, j)),
            pl.BlockSpec((tm, tn), lambda i, j: (i, j)),
            pl.BlockSpec((1, 1, tn), lambda i, j: (i // tpb, 0, j)),
            pl.BlockSpec((tm, tm), lambda i, j: (0, 0)),
            pl.BlockSpec((tm, tm), lambda i, j: (0, 0))],
        out_specs=pl.BlockSpec((tm, tn), lambda i, j: (i, j)),
        out_shape=jax.ShapeDtypeStruct((m, d), F32),
        scratch_shapes=[pltpu.VMEM((tm, k), BF16), pltpu.VMEM((tm, k), BF16),
                        pltpu.VMEM((N_GROUPS - 1, tm, k), BF16),
                        pltpu.VMEM((N_GROUPS, tm, HEAD), F32)],
        compiler_params=_cparams("parallel", "arbitrary"),
        name="merge_matmul_gated_residual",
    )(*outs, *lses, w, x, gate.reshape(bsz, 1, d),
      jnp.asarray(_residue_major_perm(tm, DILATIONS[1]).T, BF16),
      jnp.asarray(_residue_major_perm(tm, DILATIONS[2]).T, BF16))


def _mlp_kernel(x_ref, sh_ref, sc_ref, gt_ref, g_ref, w1_ref, w2_ref, o_ref, h_ref, acc_ref,
                a_ref, *, nf):
    f = pl.program_id(1)

    def up():
        a = jnp.dot(h_ref[...], w1_ref[...], preferred_element_type=F32)
        return jnp.square(jnp.maximum(a, 0.0)).astype(BF16)

    def down(slot):
        acc_ref[...] += jnp.dot(a_ref[slot], w2_ref[...], preferred_element_type=F32)

    @pl.when(f == 0)
    def _():
        h_ref[...] = _norm_modulate(x_ref[...], g_ref[...], sh_ref[0], sc_ref[0]).astype(BF16)
        acc_ref[...] = jnp.zeros_like(acc_ref)
        a_ref[0] = up()

    for slot in range(2):
        @pl.when((f > 0) & (f < nf) & (f % 2 == slot))
        def _(slot=slot):
            a_new = up()
            down(1 - slot)
            a_ref[slot] = a_new

    @pl.when(f == nf)
    def _():
        down((nf - 1) % 2)
        o_ref[...] = x_ref[...] + gt_ref[0] * acc_ref[...]


def mlp_sublayer(x, gain, shift, scale, gate, w1, w2, seq, *, tm=512, tf=1024):
    m, d = x.shape
    dff = w1.shape[1]
    nf = dff // tf
    bsz = gate.shape[0]
    tpb = seq // tm
    vec = pl.BlockSpec((1, 1, d), lambda i, f: (i // tpb, 0, 0))
    return pl.pallas_call(
        functools.partial(_mlp_kernel, nf=nf),
        grid=(m // tm, nf + 1),
        in_specs=[
            pl.BlockSpec((tm, d), lambda i, f: (i, 0)),
            vec, vec, vec,
            pl.BlockSpec((1, d), lambda i, f: (0, 0)),
            pl.BlockSpec((d, tf), lambda i, f: (0, jnp.minimum(f, nf - 1))),
            pl.BlockSpec((tf, d), lambda i, f: (jnp.maximum(f - 1, 0), 0)),
        ],
        out_specs=pl.BlockSpec((tm, d), lambda i, f: (i, 0)),
        out_shape=jax.ShapeDtypeStruct((m, d), F32),
        scratch_shapes=[pltpu.VMEM((tm, d), BF16), pltpu.VMEM((tm, d), F32),
                        pltpu.VMEM((2, tm, tf), BF16)],
        compiler_params=_cparams("parallel", "arbitrary"),
        name="mlp_sublayer",
    )(x, shift.reshape(bsz, 1, d), scale.reshape(bsz, 1, d), gate.reshape(bsz, 1, d),
      gain.reshape(1, d), w1, w2)


def _softplus(v):
    return jnp.maximum(v, 0.0) + jnp.log1p(jnp.exp(-jnp.abs(v)))


def _gates_kernel(ba_ref, alog_ref, dtb_ref, tril_ref, g_ref, gt_ref):
    ba = ba_ref[...]
    lane = lax.broadcasted_iota(jnp.int32, ba.shape, 1)
    is_alpha = (lane >= GDN_NV) & (lane < 2 * GDN_NV)
    g = jnp.where(is_alpha, -jnp.exp(alog_ref[...]) * _softplus(ba + dtb_ref[...]), 0.0)
    gc = jnp.dot(tril_ref[...], g, preferred_element_type=F32, precision=lax.Precision.HIGHEST)
    out = jnp.where(lane < GDN_NV, jax.nn.sigmoid(ba), gc)
    g_ref[...] = out
    gt_ref[...] = out.T


def gdn_gates(proj, a_log, dt_bias, tg=512):
    m = proj.shape[0]
    tg = min(tg, m)
    alog_row = jnp.zeros((1, HEAD), F32).at[0, GDN_NV:2 * GDN_NV].set(a_log)
    dtb_row = jnp.zeros((1, HEAD), F32).at[0, GDN_NV:2 * GDN_NV].set(dt_bias)
    idx = np.arange(tg)
    tril = ((idx[:, None] >= idx[None, :]) & (idx[:, None] // CHUNK == idx[None, :] // CHUNK))
    return pl.pallas_call(
        _gates_kernel,
        grid=(m // tg,),
        in_specs=[pl.BlockSpec((tg, HEAD), lambda i: (i, GDN_BA_COL // HEAD)),
                  pl.BlockSpec((1, HEAD), lambda i: (0, 0)),
                  pl.BlockSpec((1, HEAD), lambda i: (0, 0)),
                  pl.BlockSpec((tg, tg), lambda i: (0, 0))],
        out_specs=[pl.BlockSpec((tg, HEAD), lambda i: (i, 0)),
                   pl.BlockSpec((HEAD, tg), lambda i: (0, i))],
        out_shape=[jax.ShapeDtypeStruct((m, HEAD), F32), jax.ShapeDtypeStruct((HEAD, m), F32)],
        compiler_params=_cparams("parallel"),
        name="gdn_gates",
    )(proj, alog_row, dtb_row, jnp.asarray(tril.astype(np.float32)))


def _dot_t(a, b):
    return lax.dot_general(a, b, (((1,), (1,)), ((), ())), preferred_element_type=F32)


def _tdot(a, b):
    return lax.dot_general(a, b, (((0,), (0,)), ((), ())), preferred_element_type=F32)


def _bmm(a, b):
    return jnp.einsum("bij,bjk->bik", a.astype(BF16), b.astype(BF16),
                      preferred_element_type=F32)


def _bmm_t(a, b):
    return jnp.einsum("bik,bjk->bij", a.astype(BF16), b.astype(BF16),
                      preferred_element_type=F32)


def _pair_blockdiag(m2):
    first = lax.broadcasted_iota(jnp.int32, m2.shape[1:], 1) < CHUNK
    return jnp.concatenate([jnp.where(first, m2, 0.0), jnp.where(first, 0.0, m2)], axis=1)


def _unit_lower_inverse_pairs(l2):
    i = lax.broadcasted_iota(jnp.int32, (CHUNK, 2 * CHUNK), 0)
    j = lax.broadcasted_iota(jnp.int32, (CHUNK, 2 * CHUNK), 1) & (CHUNK - 1)

    def sub_blocks(s):
        return (((i >> s) & 1) == 1) & ((j >> s) == (i >> s) - 1)

    inv = jnp.where(i == j, 1.0, 0.0) - jnp.where(sub_blocks(0), l2, 0.0)
    s = 1
    while (1 << s) < CHUNK:
        x = _bmm(inv, _pair_blockdiag(jnp.where(sub_blocks(s), l2, 0.0)))
        inv = inv - _bmm(x, _pair_blockdiag(inv))
        s += 1
    return inv


def _gdn_kernel(q_ref, k_ref, v_ref, z_ref, wq_ref, wk_ref, wv_ref, g_ref, gt_ref, og_ref,
                o_ref, qbuf, kbuf, vbuf, state_ref, *, tt, hg):
    hgi = pl.program_id(1)
    t = pl.program_id(2)
    nc = tt // CHUNK
    nh = 2 * hg

    @pl.when(t == 0)
    def _():
        state_ref[...] = jnp.zeros_like(state_ref)
        qbuf[pl.ds(tt, 8), :] = jnp.zeros((8, hg * HEAD), F32)
        kbuf[pl.ds(tt, 8), :] = jnp.zeros((8, hg * HEAD), F32)
        vbuf[pl.ds(tt, 8), :] = jnp.zeros((8, nh * HEAD), F32)

    def conv_silu(x_ref, buf, w_ref):
        buf[pl.ds(0, 8), :] = buf[pl.ds(tt, 8), :]
        buf[pl.ds(8, tt), :] = x_ref[0]
        acc = w_ref[GDN_CONV_K - 1:GDN_CONV_K, :] * buf[pl.ds(8, tt), :]
        for j in range(1, GDN_CONV_K):
            tap = GDN_CONV_K - 1 - j
            acc = acc + w_ref[tap:tap + 1, :] * buf[pl.ds(8 - j, tt), :]
        return _silu(acc)

    def l2n(v):
        return v * lax.rsqrt(jnp.sum(v * v, axis=-1, keepdims=True) + NORM_EPS)

    def chunked(v):
        return v.reshape(nc, CHUNK, HEAD)

    qa = conv_silu(q_ref, qbuf, wq_ref)
    ka = conv_silu(k_ref, kbuf, wk_ref)
    va = conv_silu(v_ref, vbuf, wv_ref)

    gates = g_ref[0]
    lane = lax.broadcasted_iota(jnp.int32, gates.shape, 1)
    first_tt = lane < CHUNK
    row_i = lax.broadcasted_iota(jnp.int32, (CHUNK, 2 * CHUNK), 0)
    col_j = lax.broadcasted_iota(jnp.int32, (CHUNK, 2 * CHUNK), 1) & (CHUNK - 1)
    causal = row_i >= col_j
    strict = row_i > col_j
    zeros_rhs = jnp.zeros((nc, CHUNK, 2 * HEAD), F32)

    lmats, rhss, qgs, kgs, pmats, elasts = [], [], [], [], [], []
    for khl in range(hg):
        hs = slice(khl * HEAD, (khl + 1) * HEAD)
        qc = chunked(l2n(qa[:, hs]) * (HEAD ** -0.5))
        kc = chunked(l2n(ka[:, hs]))
        kc2 = jnp.concatenate([kc, kc], axis=1)
        kk2 = _bmm_t(kc, kc2)
        qk2 = _bmm_t(qc, kc2)
        beta_cols, gc_cols, gc_rows = [], [], []
        for e in range(2):
            hv = hgi * nh + 2 * khl + e
            beta_cols.append(jnp.sum(jnp.where(lane == hv, gates, 0.0), axis=-1, keepdims=True))
            gc_cols.append(
                jnp.sum(jnp.where(lane == hv + GDN_NV, gates, 0.0), axis=-1, keepdims=True))
            gc_rows.append(gt_ref[pl.ds(hv + GDN_NV, 1), :])
        bi2 = jnp.where(first_tt, beta_cols[0], beta_cols[1]).reshape(nc, CHUNK, HEAD)
        gci2 = jnp.where(first_tt, gc_cols[0], gc_cols[1]).reshape(nc, CHUNK, HEAD)
        gcj2 = jnp.stack(
            [jnp.concatenate([gc_rows[0][:, c * CHUNK:(c + 1) * CHUNK],
                              gc_rows[1][:, c * CHUNK:(c + 1) * CHUNK]], axis=1)
             for c in range(nc)], axis=0)
        dec2 = jnp.exp(jnp.where(causal, gci2 - gcj2, NEG_BIG))
        lmats.append(jnp.where(strict, kk2 * bi2 * dec2, 0.0))
        pmats.append((qk2 * dec2).astype(BF16))
        rhs_pair = []
        for e in range(2):
            hl = 2 * khl + e
            bi = beta_cols[e].reshape(nc, CHUNK, 1)
            gci = gc_cols[e].reshape(nc, CHUNK, 1)
            egc = jnp.exp(gci)
            g_last = gci[:, CHUNK - 1:CHUNK, :]
            vc = chunked(va[:, hl * HEAD:(hl + 1) * HEAD])
            rhs_pair.append(jnp.concatenate([vc * bi, kc * (bi * egc)], axis=-1))
            qgs.append((qc * egc).astype(BF16))
            kgs.append((kc * jnp.exp(g_last - gci)).astype(BF16))
            elasts.append(jnp.exp(g_last))
        rhss.append(jnp.concatenate(
            [jnp.concatenate([rhs_pair[0], zeros_rhs], axis=-1),
             jnp.concatenate([zeros_rhs, rhs_pair[1]], axis=-1)], axis=1))
    inv2 = _unit_lower_inverse_pairs(jnp.concatenate(lmats, axis=0))
    sol2 = _bmm(inv2, jnp.concatenate(rhss, axis=0))

    states = [state_ref[hl] for hl in range(nh)]
    o_chunks = [[] for _ in range(nh)]
    zeros_vn = jnp.zeros((CHUNK, HEAD), BF16)
    for c in range(nc):
        for khl in range(hg):
            p = khl * nc + c
            qss, vnbs = [], []
            for e in range(2):
                hl = 2 * khl + e
                u = sol2[p, :, 2 * e * HEAD:(2 * e + 1) * HEAD]
                w = sol2[p, :, (2 * e + 1) * HEAD:(2 * e + 2) * HEAD]
                ws_qs = jnp.dot(jnp.concatenate([w.astype(BF16), qgs[hl][c]], axis=0),
                                states[hl].astype(BF16), preferred_element_type=F32)
                vnb = (u - ws_qs[:CHUNK]).astype(BF16)
                states[hl] = states[hl] * elasts[hl][c] + _tdot(kgs[hl][c], vnb)
                qss.append(ws_qs[CHUNK:])
                vnbs.append(vnb)
            vn_bd = jnp.concatenate(
                [jnp.concatenate([vnbs[0], zeros_vn], axis=-1),
                 jnp.concatenate([zeros_vn, vnbs[1]], axis=-1)], axis=0)
            pv = jnp.dot(pmats[khl][c], vn_bd, preferred_element_type=F32)
            for e in range(2):
                o_chunks[2 * khl + e].append(qss[e] + pv[:, e * HEAD:(e + 1) * HEAD])
    for hl in range(nh):
        state_ref[hl] = states[hl]
        o = jnp.concatenate(o_chunks[hl], axis=0)
        z = z_ref[0, :, hl * HEAD:(hl + 1) * HEAD]
        y = o * lax.rsqrt(jnp.mean(o * o, axis=-1, keepdims=True) + NORM_EPS) * og_ref[...]
        o_ref[0, :, hl * HEAD:(hl + 1) * HEAD] = (y * _silu(z)).astype(BF16)


def gdn_core(proj, conv_w, gates, gates_t, onorm_g, bsz, seq, *, tt=512, hg=4):
    tt = min(tt, seq)
    nt = seq // tt
    nh = 2 * hg
    proj3 = proj.reshape(bsz, seq, proj.shape[1])
    qw = hg * HEAD
    vw = nh * HEAD
    kq = GDN_QK // qw
    kv = 2 * GDN_QK // vw
    kz = GDN_CONV // vw
    out = pl.pallas_call(
        functools.partial(_gdn_kernel, tt=tt, hg=hg),
        grid=(bsz, GDN_NK // hg, nt),
        in_specs=[
            pl.BlockSpec((1, tt, qw), lambda b, h, t: (b, t, h)),
            pl.BlockSpec((1, tt, qw), lambda b, h, t: (b, t, kq + h)),
            pl.BlockSpec((1, tt, vw), lambda b, h, t: (b, t, kv + h)),
            pl.BlockSpec((1, tt, vw), lambda b, h, t: (b, t, kz + h)),
            pl.BlockSpec((GDN_CONV_K, qw), lambda b, h, t: (0, h)),
            pl.BlockSpec((GDN_CONV_K, qw), lambda b, h, t: (0, kq + h)),
            pl.BlockSpec((GDN_CONV_K, vw), lambda b, h, t: (0, kv + h)),
            pl.BlockSpec((1, tt, HEAD), lambda b, h, t: (b, t, 0)),
            pl.BlockSpec((HEAD, tt), lambda b, h, t: (0, b * nt + t)),
            pl.BlockSpec((1, HEAD), lambda b, h, t: (0, 0)),
        ],
        out_specs=pl.BlockSpec((1, tt, vw), lambda b, h, t: (b, t, h)),
        out_shape=jax.ShapeDtypeStruct((bsz, seq, GDN_V), BF16),
        scratch_shapes=[pltpu.VMEM((tt + 8, qw), F32), pltpu.VMEM((tt + 8, qw), F32),
                        pltpu.VMEM((tt + 8, vw), F32), pltpu.VMEM((nh, HEAD, HEAD), F32)],
        compiler_params=_cparams("parallel", "parallel", "arbitrary"),
        name="gdn_core",
    )(proj3, proj3, proj3, proj3, conv_w, conv_w, conv_w,
      gates.reshape(bsz, seq, HEAD), gates_t, onorm_g.reshape(1, HEAD))
    return out.reshape(bsz * seq, GDN_V)


def _attn_kernel(q_ref, kp_ref, kc_ref, vp_ref, vc_ref, o_ref, lse_ref, *, nsb):
    step = pl.program_id(2)
    rows = ATT_G * SPAN
    qi = lax.broadcasted_iota(jnp.int32, (rows, 2 * SPAN), 0) & (SPAN - 1)
    kj = lax.broadcasted_iota(jnp.int32, (rows, 2 * SPAN), 1)
    rel = qi + SPAN - kj
    band = (rel >= 0) & (rel <= SPAN)
    first_mask = band & (kj >= jnp.where(step > 0, 0, SPAN))
    scale = HEAD ** -0.5
    lse_ref[0] = jnp.zeros(lse_ref.shape[1:], F32)
    for sb in range(nsb):
        qrows = slice(sb * SPAN, (sb + 1) * SPAN)
        mask = first_mask if sb == 0 else band
        for h in range(ATT_HKV):
            hs = slice(h * HEAD, (h + 1) * HEAD)
            if sb == 0:
                kprev, vprev = kp_ref[0, :, hs], vp_ref[0, :, hs]
            else:
                kprev = kc_ref[0, (sb - 1) * SPAN:sb * SPAN, hs]
                vprev = vc_ref[0, (sb - 1) * SPAN:sb * SPAN, hs]
            k2 = jnp.concatenate([kprev, kc_ref[0, qrows, hs]], axis=0)
            v2 = jnp.concatenate([vprev, vc_ref[0, qrows, hs]], axis=0)
            q4 = jnp.concatenate(
                [q_ref[0, qrows, (h * ATT_G + g) * HEAD:(h * ATT_G + g + 1) * HEAD]
                 for g in range(ATT_G)], axis=0)
            s = jnp.where(mask, _dot_t(q4, k2) * scale, NEG_BIG)
            mx = jnp.max(s, axis=-1, keepdims=True)
            p = jnp.exp(s - mx)
            den = jnp.sum(p, axis=-1, keepdims=True)
            o4 = jnp.dot(p.astype(BF16), v2, preferred_element_type=F32) / den
            lse = mx + jnp.log(den)
            for g in range(ATT_G):
                hq = h * ATT_G + g
                o_ref[0, qrows, hq * HEAD:(hq + 1) * HEAD] = (
                    o4[g * SPAN:(g + 1) * SPAN].astype(o_ref.dtype))
                lse_ref[0, qrows, hq:hq + 1] = lse[g * SPAN:(g + 1) * SPAN]


def dilated_attention_branch(q, k, v, gi):
    dil = DILATIONS[gi]
    bsz, sub, _ = q.shape
    nsb = next(n for n in (4, 2, 1) if sub % (n * SPAN) == 0)
    tq = nsb * SPAN
    qd = ATT_HQ * HEAD
    kd = ATT_HKV * HEAD

    def cur(b, r, st):
        return (b, st, r)

    def prev(b, r, st):
        return (b, jnp.maximum(st * nsb - 1, 0), r)

    return pl.pallas_call(
        functools.partial(_attn_kernel, nsb=nsb),
        grid=(bsz, dil, sub // tq),
        in_specs=[
            pl.BlockSpec((1, tq, qd), cur),
            pl.BlockSpec((1, SPAN, kd), prev),
            pl.BlockSpec((1, tq, kd), cur),
            pl.BlockSpec((1, SPAN, kd), prev),
            pl.BlockSpec((1, tq, kd), cur),
        ],
        out_specs=[pl.BlockSpec((1, tq, qd), cur), pl.BlockSpec((1, tq, HEAD), cur)],
        out_shape=[jax.ShapeDtypeStruct((bsz, sub, dil * qd), BF16),
                   jax.ShapeDtypeStruct((bsz, sub, dil * HEAD), F32)],
        compiler_params=_cparams("parallel", "parallel", "parallel"),
        name=f"dilated_attention_{gi}",
    )(q, k, k, v, v)


def kernel(x, c, positions, ada_w, ada_b, norm_g, mlp_w1, mlp_w2, gdn_w_in, gdn_conv_w, gdn_a_log,
           gdn_dt_bias, gdn_onorm_g, gdn_w_out, kv_norm_g, kv_ada_w, kv_ada_b, w_kv, k_norm_g,
           attn_w_q, q_norm_g, attn_w_o):
    bsz, seq, d = x.shape
    depth = ada_w.shape[0]
    n_a = gdn_w_in.shape[0]
    m = bsz * seq
    xf = x.reshape(m, d)

    mod = ada_modulation(c, ada_w, ada_b)
    kv_mod = ada_modulation(c, kv_ada_w[None], kv_ada_b[None])[0]
    cos_tab, sin_tab = rope_tables(positions)

    kv_cols = w_kv.reshape(d, N_GROUPS, 2, ATT_HKV * HEAD)
    w_kv_p = jnp.concatenate([_permute_head_lanes(kv_cols[:, :, 0]), kv_cols[:, :, 1]],
                             axis=-1).reshape(d, -1).astype(BF16)
    k_gain_p = _permute_head_lanes(k_norm_g)

    kv = None
    for layer in range(depth):
        sh1, sc1, gt1, sh2, sc2, gt2 = [mod[layer, :, i * d:(i + 1) * d] for i in range(6)]
        if layer < n_a:
            w_in = jnp.pad(gdn_w_in[layer], ((0, 0), (0, GDN_PROJ_PAD - gdn_w_in.shape[2])))
            proj = norm_mod_matmul(xf, norm_g[layer, 0], sh1, sc1, w_in.astype(BF16), seq,
                                   out_dtype=F32, tm=512, tn=GDN_PROJ_PAD // 5)
            gates, gates_t = gdn_gates(proj, gdn_a_log[layer], gdn_dt_bias[layer])
            o = gdn_core(proj, gdn_conv_w[layer], gates, gates_t, gdn_onorm_g[layer], bsz, seq)
            xf = matmul_gated_residual(o, gdn_w_out[layer].astype(BF16), xf, gt1, seq)
        else:
            if kv is None:
                kv = branch_projection(xf, kv_norm_g, kv_mod[:, :d], kv_mod[:, d:], w_kv_p,
                                       k_gain_p, cos_tab, sin_tab,
                                       [(ATT_HKV, True), (ATT_HKV, False)], bsz, seq)
            j = layer - n_a
            qs = branch_projection(xf, norm_g[layer, 0], sh1, sc1,
                                   _permute_head_lanes(attn_w_q[j]).astype(BF16),
                                   _permute_head_lanes(q_norm_g[j]), cos_tab, sin_tab,
                                   [(ATT_HQ, True)], bsz, seq)
            branches = [dilated_attention_branch(qs[gi], kv[2 * gi], kv[2 * gi + 1], gi)
                        for gi in range(N_GROUPS)]
            xf = merge_matmul_gated_residual([b[0] for b in branches], [b[1] for b in branches],
                                             attn_w_o[j].astype(BF16), xf, gt1, seq)
        xf = mlp_sublayer(xf, norm_g[layer, 1], sh2, sc2, gt2, mlp_w1[layer].astype(BF16),
                          mlp_w2[layer].astype(BF16), seq)
    return xf.reshape(bsz, seq, d)
```
